```python
import math
import jax, jax.numpy as jnp
from jax import lax
import numpy as np

D_MODEL = 2048
BATCH = 8
SEQ = 2048
DEPTH = 4
DEC_BATCH = 8
DEC_SEQ = 32
PAST_LEN = 2048

CHUNK = 64
N_META = 16
D_MIX = D_MODEL
D_A = D_MIX // 2
N_BLOCKS_A = 16
BS_A = D_A // N_BLOCKS_A
CONV_W = 4
RG_C = 8.0
N_HEADS_B = 8
DK = 64
DV = 2 * DK
D_B = N_HEADS_B * DV
QK_W = N_HEADS_B * 2 * DK
IN_COLS = 2 * D_A + 2 * QK_W + 2 * D_B
NUM_BUCKETS = 32
REL_MAX_DIST = 1024
QBLOCK = 128
EPS = 1e-6

kernel_name = "hymba_rglru_diffattn_stream_step"


def rms_norm(x, g):
    xf = x.astype(jnp.float32)
    y = xf * lax.rsqrt(jnp.mean(xf * xf, axis=-1, keepdims=True) + EPS)
    return (y * g.astype(jnp.float32)).astype(x.dtype)


def rel_bucket(rel):
    half = NUM_BUCKETS // 2
    max_exact = half // 2
    ret = jnp.where(rel > 0, half, 0).astype(jnp.int32)
    n = jnp.abs(rel).astype(jnp.int32)
    nf = jnp.maximum(n, 1).astype(jnp.float32)
    large = max_exact + (jnp.log(nf / max_exact) / math.log(REL_MAX_DIST / max_exact)
                         * (half - max_exact)).astype(jnp.int32)
    large = jnp.minimum(large, half - 1)
    return ret + jnp.where(n < max_exact, n, large)


def rel_bias_block(qpos, kpos, rel_bias):
    b = rel_bias.astype(jnp.float32)[rel_bucket(kpos[None, :] - qpos[:, None])]
    return jnp.transpose(b, (2, 0, 1))


def chunk_id(pos):
    return jnp.where(pos < N_META, -1, (pos - N_META) // CHUNK)


def diff_attend(q1, q2, k1, k2, v, bias, mask, lam):
    scale = DK ** -0.5
    l1 = jnp.einsum('bqhd,bkhd->bhqk', q1, k1).astype(jnp.float32) * scale + bias
    l2 = jnp.einsum('bqhd,bkhd->bhqk', q2, k2).astype(jnp.float32) * scale + bias
    if mask is not None:
        l1 = jnp.where(mask, l1, -jnp.inf)
        l2 = jnp.where(mask, l2, -jnp.inf)
    p = jax.nn.softmax(l1, axis=-1) - lam * jax.nn.softmax(l2, axis=-1)
    return jnp.einsum('bhqk,bkhd->bqhd', p.astype(v.dtype), v)


def diff_attn_prompt(q1, q2, k1, k2, v, rel_bias, lam):
    B, L = q1.shape[0], q1.shape[1]
    nblk = -(-L // QBLOCK)
    Lp = nblk * QBLOCK
    pad = ((0, 0), (0, Lp - L), (0, 0), (0, 0))
    q1p = jnp.pad(q1, pad)
    q2p = jnp.pad(q2, pad)
    kpos = jnp.arange(L)
    kchunk = chunk_id(kpos)

    def block(i):
        s = i * QBLOCK
        qpos = s + jnp.arange(QBLOCK)
        qb1 = lax.dynamic_slice_in_dim(q1p, s, QBLOCK, axis=1)
        qb2 = lax.dynamic_slice_in_dim(q2p, s, QBLOCK, axis=1)
        bias = rel_bias_block(qpos, kpos, rel_bias)
        mask = kchunk[None, :] <= chunk_id(qpos)[:, None]
        return diff_attend(qb1, qb2, k1, k2, v, bias, mask, lam)

    out = lax.map(block, jnp.arange(nblk))
    out = jnp.moveaxis(out, 0, 1).reshape(B, Lp, N_HEADS_B, DV)
    return out[:, :L]


def diff_attn_sample(q1, q2, k1_all, k2_all, v_all, rel_bias, lam):
    T = q1.shape[1]
    K = k1_all.shape[1]
    P = K - T
    kpos = jnp.arange(K)
    qpos = P + jnp.arange(T)
    bias = rel_bias_block(qpos, kpos, rel_bias)
    return diff_attend(q1, q2, k1_all, k2_all, v_all, bias, None, lam)


def lin_scan(a, b, h0):
    b = b.at[:, 0].add(a[:, 0] * h0)

    def comb(left, right):
        return (left[0] * right[0], right[0] * left[1] + right[1])

    _, h = lax.associative_scan(comb, (a, b), axis=1)
    return h


def rglru_branch(xa, conv_buf, h0, conv_w, conv_b, wr, br, wi, bi, lam_param):
    B, T = xa.shape[0], xa.shape[1]
    xp = jnp.concatenate([conv_buf.astype(xa.dtype), xa], axis=1)
    xc = conv_b
    for j in range(CONV_W):
        xc = xc + xp[:, j:j + T] * conv_w[j]
    new_buf = xp[:, -(CONV_W - 1):]
    xb = xc.reshape(B, T, N_BLOCKS_A, BS_A)
    r = jax.nn.sigmoid(jnp.einsum('btnc,ncd->btnd', xb, wr) + br).reshape(B, T, D_A)
    i = jax.nn.sigmoid(jnp.einsum('btnc,ncd->btnd', xb, wi) + bi).reshape(B, T, D_A)
    log_a = -RG_C * r.astype(jnp.float32) * jax.nn.softplus(-lam_param.astype(jnp.float32))
    a = jnp.exp(log_a)
    bterm = jnp.sqrt(-jnp.expm1(2.0 * log_a)) * (i * xc).astype(jnp.float32)
    h = lin_scan(a, bterm, h0.astype(jnp.float32))
    return h.astype(xa.dtype), new_buf, h[:, -1]


def layer_forward(x, conv_buf, h0, k_cache, v_cache, rel_bias, pre_g, post_g, w_in, conv_w, conv_b,
                  wr, br, wi, bi, rglru_lam, lq1, lk1, lq2, lk2, subln_g, w_out, lam_init, is_prompt):
    B, T = x.shape[0], x.shape[1]
    u = rms_norm(x, pre_g)
    proj = jnp.einsum('btd,dc->btc', u, w_in)
    splits = [D_A, 2 * D_A, 2 * D_A + QK_W, 2 * D_A + 2 * QK_W, 2 * D_A + 2 * QK_W + D_B]
    xa, ga, q, k, v, gb = jnp.split(proj, splits, axis=-1)
    ya, new_buf, h_last = rglru_branch(xa, conv_buf, h0, conv_w, conv_b, wr, br, wi, bi, rglru_lam)
    ya = ya * jax.nn.silu(ga)
    q = q.reshape(B, T, N_HEADS_B, 2, DK)
    k_rows = k.reshape(B, T, N_HEADS_B, 2 * DK)
    v_rows = v.reshape(B, T, N_HEADS_B, DV)
    lam = (jnp.exp(jnp.sum(lq1.astype(jnp.float32) * lk1.astype(jnp.float32)))
           - jnp.exp(jnp.sum(lq2.astype(jnp.float32) * lk2.astype(jnp.float32))) + lam_init)
    if is_prompt:
        o = diff_attn_prompt(q[..., 0, :], q[..., 1, :], k_rows[..., :DK], k_rows[..., DK:], v_rows,
                             rel_bias, lam)
    else:
        k_all = jnp.concatenate([k_cache.astype(k_rows.dtype), k_rows], axis=1)
        v_all = jnp.concatenate([v_cache.astype(v_rows.dtype), v_rows], axis=1)
        o = diff_attn_sample(q[..., 0, :], q[..., 1, :], k_all[..., :DK], k_all[..., DK:], v_all,
                             rel_bias, lam)
    o = rms_norm(o, subln_g) * (1.0 - lam_init)
    yb = o.reshape(B, T, D_B) * jax.nn.silu(gb)
    y = jnp.einsum('btc,cd->btd', jnp.concatenate([ya, yb], axis=-1), w_out)
    x = x + rms_norm(y, post_g)
    return x, k_rows, v_rows, new_buf, h_last


def setup_inputs(seed: int = 0) -> dict:
    key = jax.random.key(seed)
    ks = jax.random.split(key, 24)
    f32 = jnp.float32
    nrm = lambda k, shape, s: (jax.random.normal(k, shape, f32) * s)
    u = jax.random.uniform(ks[12], (DEPTH, D_A), f32, 0.9, 0.999)
    s = u ** (1.0 / RG_C)
    rglru_lam = jnp.log(s / (1.0 - s))
    return {
        "x_prompt": nrm(ks[0], (BATCH, SEQ, D_MODEL), 1.0),
        "x_sample": nrm(ks[1], (DEC_BATCH, DEC_SEQ, D_MODEL), 1.0),
        "cache_k": nrm(ks[2], (DEPTH, DEC_BATCH, PAST_LEN, N_HEADS_B, 2 * DK), 1.0),
        "cache_v": nrm(ks[3], (DEPTH, DEC_BATCH, PAST_LEN, N_HEADS_B, DV), 1.0),
        "state_conv": nrm(ks[4], (DEPTH, DEC_BATCH, CONV_W - 1, D_A), 1.0),
        "state_rglru": nrm(ks[5], (DEPTH, DEC_BATCH, D_A), 0.5),
        "meta": nrm(ks[6], (N_META, D_MODEL), 1.0),
        "rel_bias": nrm(ks[7], (NUM_BUCKETS, N_HEADS_B), 0.5),
        "pre_g": 1.0 + nrm(ks[8], (DEPTH, D_MODEL), 0.05),
        "post_g": 1.0 + nrm(ks[9], (DEPTH, D_MODEL), 0.05),
        "w_in": nrm(ks[10], (DEPTH, D_MODEL, IN_COLS), D_MODEL ** -0.5),
        "conv_w": nrm(ks[11], (DEPTH, CONV_W, D_A), CONV_W ** -0.5),
        "conv_b": nrm(ks[13], (DEPTH, D_A), 0.02),
        "gate_r_w": nrm(ks[14], (DEPTH, N_BLOCKS_A, BS_A, BS_A), BS_A ** -0.5),
        "gate_r_b": nrm(ks[15], (DEPTH, N_BLOCKS_A, BS_A), 0.02),
        "gate_i_w": nrm(ks[16], (DEPTH, N_BLOCKS_A, BS_A, BS_A), BS_A ** -0.5),
        "gate_i_b": nrm(ks[17], (DEPTH, N_BLOCKS_A, BS_A), 0.02),
        "rglru_lam": rglru_lam,
        "lam_q1": nrm(ks[18], (DEPTH, DK), 0.1),
        "lam_k1": nrm(ks[19], (DEPTH, DK), 0.1),
        "lam_q2": nrm(ks[20], (DEPTH, DK), 0.1),
        "lam_k2": nrm(ks[21], (DEPTH, DK), 0.1),
        "subln_g": 1.0 + nrm(ks[22], (DEPTH, DV), 0.05),
        "w_out": nrm(ks[23], (DEPTH, D_MIX, D_MODEL), D_MIX ** -0.5),
    }


def reference(x_prompt, x_sample, cache_k, cache_v, state_conv, state_rglru, meta, rel_bias,
              pre_g, post_g, w_in, conv_w, conv_b, gate_r_w, gate_r_b, gate_i_w, gate_i_b,
              rglru_lam, lam_q1, lam_k1, lam_q2, lam_k2, subln_g, w_out):
    B = x_prompt.shape[0]
    hp = jnp.concatenate([jnp.broadcast_to(meta.astype(x_prompt.dtype)[None], (B, N_META, D_MODEL)),
                          x_prompt], axis=1)
    hs = x_sample
    zero_buf = jnp.zeros((B, CONV_W - 1, D_A), x_prompt.dtype)
    zero_h = jnp.zeros((B, D_A), jnp.float32)
    kp_l, vp_l, cp_l, rp_l = [], [], [], []
    ks_l, vs_l, cs_l, rs_l = [], [], [], []
    for l in range(DEPTH):
        lam_init = 0.8 - 0.6 * math.exp(-0.3 * l)
        params = (rel_bias, pre_g[l], post_g[l], w_in[l], conv_w[l], conv_b[l], gate_r_w[l], gate_r_b[l],
                  gate_i_w[l], gate_i_b[l], rglru_lam[l], lam_q1[l], lam_k1[l], lam_q2[l], lam_k2[l],
                  subln_g[l], w_out[l], lam_init)
        hp, kp, vp, cp, rp = layer_forward(hp, zero_buf, zero_h, None, None, *params, True)
        hs, kk, vv, cs, rs = layer_forward(hs, state_conv[l], state_rglru[l], cache_k[l], cache_v[l],
                                           *params, False)
        kp_l.append(kp); vp_l.append(vp); cp_l.append(cp); rp_l.append(rp)
        ks_l.append(kk); vs_l.append(vv); cs_l.append(cs); rs_l.append(rs)
    y_prompt = hp[:, N_META:]
    y_sample = hs
    return (y_prompt, y_sample,
            jnp.stack(kp_l), jnp.stack(vp_l), jnp.stack(cp_l), jnp.stack(rp_l),
            jnp.stack(ks_l), jnp.stack(vs_l), jnp.stack(cs_l), jnp.stack(rs_l))
```

```python
import functools
import math

import jax
import jax.numpy as jnp
from jax import lax
from jax.experimental import pallas as pl
from jax.experimental.pallas import tpu as pltpu

F32 = jnp.float32
BF16 = jnp.bfloat16

CHUNK = 64
N_META = 16
N_BLOCKS_A = 16
CONV_W = 4
RG_C = 8.0
N_HEADS_B = 8
DK = 64
DV = 2 * DK
NUM_BUCKETS = 32
REL_MAX_DIST = 1024
EPS = 1e-6

LANES = 128
SUBLANES = 8
MXU_DIM = 256
VMEM_LIMIT_BYTES = 56 * 1024 * 1024

Q_TILE = 256
GATE_BLOCK = 256


def _sigmoid(x):
    return 1.0 / (1.0 + jnp.exp(-x))


def _row_tile(m, cap=512):
    best = None
    for t in range(16, min(m, cap) + 1, 16):
        if m % t == 0:
            best = t
    assert best is not None, m
    return best


def _bias_lookup(rel, tab_ref, h):
    half = NUM_BUCKETS // 2
    max_exact = half // 2
    ret = jnp.where(rel > 0, half, 0).astype(jnp.int32)
    n = jnp.abs(rel)
    nf = jnp.maximum(n, 1).astype(F32)
    large = max_exact + (jnp.log(nf / max_exact) / math.log(REL_MAX_DIST / max_exact)
                         * (half - max_exact)).astype(jnp.int32)
    large = jnp.minimum(large, half - 1)
    bucket = ret + jnp.where(n < max_exact, n, large)
    val = jnp.zeros(rel.shape, F32)
    for b in range(NUM_BUCKETS):
        val = jnp.where(bucket == b, tab_ref[b, h], val)
    return val


def _prompt_slab_kernel(tab_ref, out_ref, *, n_tiles):
    h = pl.program_id(0)
    tp = (2 * n_tiles + 1) * LANES
    diag0 = Q_TILE * (n_tiles - 1) + N_META
    blk = 128
    for r0 in range(0, tp, blk):
        r = lax.broadcasted_iota(jnp.int32, (blk, Q_TILE), 0) + r0
        c = lax.broadcasted_iota(jnp.int32, (blk, Q_TILE), 1)
        val = _bias_lookup(r - c - diag0, tab_ref, h)
        rp = r - diag0
        masked = jnp.logical_and(rp >= 0, (rp // CHUNK) > (c // CHUNK))
        out_ref[r0:r0 + blk, :] = jnp.where(masked, -jnp.inf, val)
    k = lax.broadcasted_iota(jnp.int32, (LANES, Q_TILE), 0)
    c = lax.broadcasted_iota(jnp.int32, (LANES, Q_TILE), 1)
    val = _bias_lookup(k - c, tab_ref, h)
    out_ref[tp:tp + LANES, :] = jnp.where(k >= N_META, -jnp.inf, val)


def _sample_slab_kernel(tab_ref, out_ref, *, past, dec_seq):
    h = pl.program_id(0)
    width = out_ref.shape[-1]
    for c0 in range(0, width, 512):
        w = min(512, width - c0)
        t = lax.broadcasted_iota(jnp.int32, (dec_seq, w), 0)
        k = lax.broadcasted_iota(jnp.int32, (dec_seq, w), 1) + c0
        val = _bias_lookup(k - past - t, tab_ref, h)
        out_ref[:, c0:c0 + w] = jnp.where(k >= past + dec_seq, -jnp.inf, val)


def _bias_slabs(rel_bias, n_tiles, past, dec_seq):
    tp = (2 * n_tiles + 1) * LANES
    smem = pl.BlockSpec(memory_space=pltpu.SMEM)
    prompt = pl.pallas_call(
        functools.partial(_prompt_slab_kernel, n_tiles=n_tiles),
        grid=(N_HEADS_B,),
        in_specs=[smem],
        out_specs=pl.BlockSpec((None, tp + LANES, Q_TILE), lambda h: (h, 0, 0)),
        out_shape=jax.ShapeDtypeStruct((N_HEADS_B, tp + LANES, Q_TILE), F32),
        name="prompt_bias_slab",
    )(rel_bias)
    width = past + LANES
    sample = pl.pallas_call(
        functools.partial(_sample_slab_kernel, past=past, dec_seq=dec_seq),
        grid=(N_HEADS_B,),
        in_specs=[smem],
        out_specs=pl.BlockSpec((None, dec_seq, width), lambda h: (h, 0, 0)),
        out_shape=jax.ShapeDtypeStruct((N_HEADS_B, dec_seq, width), F32),
        name="sample_bias_slab",
    )(rel_bias)
    return prompt, sample


def _inproj_kernel(x_ref, g_ref, w_ref, *rest, col_chunk):
    xa_ref, ga_ref, q_ref, k_ref, v_ref, gb_ref, u_s = rest[-7:]
    x = x_ref[...]
    ms = jnp.mean(x * x, axis=-1, keepdims=True)
    u_s[...] = ((x * lax.rsqrt(ms + EPS)) * g_ref[...]).astype(BF16)
    col = 0
    for ref, scale in ((xa_ref, None), (ga_ref, None), (q_ref, DK ** -0.5),
                       (k_ref, None), (v_ref, None), (gb_ref, None)):
        width = ref.shape[-1]
        for c in range(0, width, col_chunk):
            acc = jnp.dot(u_s[...], w_ref[:, col + c:col + c + col_chunk], preferred_element_type=F32)
            if scale is not None:
                acc = acc * scale
            ref[:, c:c + col_chunk] = acc.astype(ref.dtype)
        col += width


def _inproj(x2d, pre_g, w_in_b, layer, kv_stacks, depth):
    m, d = x2d.shape
    d_a = d // 2
    qk_w = N_HEADS_B * 2 * DK
    d_b = N_HEADS_B * DV
    in_cols = w_in_b.shape[-1]
    tm = _row_tile(m, 384)
    row = lambda width: pl.BlockSpec((tm, width), lambda i: (i, 0))
    stack = lambda width: pl.BlockSpec((None, tm, width), lambda i: (layer, i, 0))
    in_specs = [
        row(d),
        pl.BlockSpec((None, 1, d), lambda i: (layer, 0, 0)),
        pl.BlockSpec((None, d, in_cols), lambda i: (layer, 0, 0), pipeline_mode=pl.Buffered(1)),
    ]
    args = [x2d, pre_g, w_in_b]
    aliases = {}
    if kv_stacks is not None:
        in_specs += [pl.BlockSpec(memory_space=pl.ANY)] * 2
        args += list(kv_stacks)
        aliases = {3: 3, 4: 4}
    out_shape = [
        jax.ShapeDtypeStruct((m, d_a), F32),
        jax.ShapeDtypeStruct((m, d_a), F32),
        jax.ShapeDtypeStruct((m, qk_w), BF16),
        jax.ShapeDtypeStruct((depth, m, qk_w), F32),
        jax.ShapeDtypeStruct((depth, m, d_b), F32),
        jax.ShapeDtypeStruct((m, d_b), F32),
    ]
    out_specs = [row(d_a), row(d_a), row(qk_w), stack(qk_w), stack(d_b), row(d_b)]
    return pl.pallas_call(
        functools.partial(_inproj_kernel, col_chunk=512),
        grid=(m // tm,),
        in_specs=in_specs,
        out_specs=out_specs,
        out_shape=out_shape,
        scratch_shapes=[pltpu.VMEM((tm, d), BF16)],
        input_output_aliases=aliases,
        compiler_params=pltpu.CompilerParams(
            dimension_semantics=("arbitrary",), vmem_limit_bytes=VMEM_LIMIT_BYTES),
        name="in_projection",
    )(*args)


def _scan8(a, b, rowid):
    for s in (1, 2, 4):
        a_sh = jnp.where(rowid >= s, pltpu.roll(a, s, 0), 1.0)
        b_sh = jnp.where(rowid >= s, pltpu.roll(b, s, 0), 0.0)
        b = a * b_sh + b
        a = a * a_sh
    return a, b


def _rglru_kernel(xa_ref, ga_ref, buf_ref, h0_ref, cw_ref, cb_ref, wr_ref, br_ref, wi_ref, bi_ref,
                  lam_ref, y_ref, nbuf_ref, hl_ref, xp_s, a_s, b_s, *, seq, row_block):
    c = xa_ref.shape[-1]
    xp_s[0:SUBLANES, :] = buf_ref[...]
    xp_s[SUBLANES:SUBLANES + seq, :] = xa_ref[...]
    nbuf_ref[...] = xp_s[seq:seq + SUBLANES, :]

    lam = lam_ref[...]
    neg = -lam
    softplus = jnp.maximum(neg, 0.0) + jnp.log1p(jnp.exp(-jnp.abs(neg)))
    cneg = -RG_C * softplus
    cb = cb_ref[...]
    taps = [cw_ref[j:j + 1, :] for j in range(CONV_W)]

    for r0 in range(0, seq, row_block):
        base = r0 + SUBLANES - (CONV_W - 1)
        xc = cb
        for j in range(CONV_W):
            xc = xc + xp_s[base + j:base + j + row_block, :] * taps[j]
        xcb = xc.astype(BF16)
        r = _sigmoid(jnp.dot(xcb, wr_ref[...], preferred_element_type=F32) + br_ref[...])
        ig = _sigmoid(jnp.dot(xcb, wi_ref[...], preferred_element_type=F32) + bi_ref[...])
        log_a = r * cneg
        a = jnp.exp(log_a)
        a_s[r0:r0 + row_block, :] = a
        b_s[r0:r0 + row_block, :] = jnp.sqrt(-jnp.tanh(log_a) * (a * a + 1.0)) * (ig * xc)

    rowid = lax.broadcasted_iota(jnp.int32, (SUBLANES, c), 0)
    group = 2 * SUBLANES

    def body(g, h):
        r = pl.multiple_of(g * group, group)
        a16 = a_s[pl.ds(r, group), :]
        b16 = b_s[pl.ds(r, group), :]
        hs = []
        for half in range(2):
            ac, bc = _scan8(a16[half * SUBLANES:(half + 1) * SUBLANES],
                            b16[half * SUBLANES:(half + 1) * SUBLANES], rowid)
            h8 = ac * h + bc
            hs.append(h8)
            h = jnp.broadcast_to(h8[SUBLANES - 1:SUBLANES, :], (SUBLANES, c))
        h16 = jnp.concatenate(hs, axis=0)
        ga = ga_ref[pl.ds(r, group), :]
        y_ref[pl.ds(r, group), :] = (h16 * (ga * _sigmoid(ga))).astype(y_ref.dtype)
        return h

    h_init = jnp.broadcast_to(h0_ref[...], (SUBLANES, c))
    h_fin = lax.fori_loop(0, seq // group, body, h_init)
    hl_ref[...] = h_fin[0:1, :]


def _rglru(xa, ga, buf8, h0, conv_w, conv_b, wr_bd, br, wi_bd, bi, lam, layer, batch, seq):
    m, d_a = xa.shape
    xa3 = xa.reshape(batch, seq, d_a)
    ga3 = ga.reshape(batch, seq, d_a)
    nc = d_a // GATE_BLOCK
    if seq % 344 == 0:
        row_block = 344
    else:
        row_block = _row_tile(seq, 512)
    chan = lambda rows: pl.BlockSpec((None, rows, GATE_BLOCK), lambda b, c: (b, 0, c))
    par = lambda rows: pl.BlockSpec((None, rows, GATE_BLOCK), lambda b, c: (layer, 0, c))
    wspec = pl.BlockSpec((None, None, GATE_BLOCK, GATE_BLOCK), lambda b, c: (layer, c, 0, 0))
    y, nbuf, hl = pl.pallas_call(
        functools.partial(_rglru_kernel, seq=seq, row_block=row_block),
        grid=(batch, nc),
        in_specs=[chan(seq), chan(seq), chan(SUBLANES), chan(1), par(CONV_W), par(1),
                  wspec, par(1), wspec, par(1), par(1)],
        out_specs=[chan(seq), chan(SUBLANES), chan(1)],
        out_shape=[jax.ShapeDtypeStruct((batch, seq, d_a), BF16),
                   jax.ShapeDtypeStruct((batch, SUBLANES, d_a), F32),
                   jax.ShapeDtypeStruct((batch, 1, d_a), F32)],
        scratch_shapes=[pltpu.VMEM((seq + SUBLANES, GATE_BLOCK), F32),
                        pltpu.VMEM((seq, GATE_BLOCK), F32),
                        pltpu.VMEM((seq, GATE_BLOCK), F32)],
        compiler_params=pltpu.CompilerParams(
            dimension_semantics=("arbitrary", "arbitrary"), vmem_limit_bytes=VMEM_LIMIT_BYTES),
        name="rglru_mixer",
    )(xa3, ga3, buf8, h0, conv_w, conv_b, wr_bd, br, wi_bd, bi, lam)
    return y.reshape(m, d_a), nbuf[:, SUBLANES - (CONV_W - 1):, :], hl[:, 0, :]


def _diff_lambda(lv, lam_init):
    s1 = jnp.sum(lv[0:1, :] * lv[1:2, :], axis=-1, keepdims=True)
    s2 = jnp.sum(lv[2:3, :] * lv[3:4, :], axis=-1, keepdims=True)
    return jnp.exp(s1) - jnp.exp(s2) + lam_init


def _attn_prompt_kernel(lv_ref, q_ref, k_ref, v_ref, gb_ref, bt_ref, g_ref, o_ref,
                        kb_s, q1_s, q2_s, vp_s, vt_s, s_s, p_s, *, seq, n_tiles, lam_init):
    tp = (2 * n_tiles + 1) * LANES
    lam = _diff_lambda(lv_ref[...], lam_init)

    kb_s[0:seq, :] = k_ref[...].astype(BF16)
    kb_s[seq:tp, :] = jnp.zeros((tp - seq, DV), BF16)
    lane = lax.broadcasted_iota(jnp.int32, (seq, 2 * DK), 1)
    qv = q_ref[...]
    zero = jnp.zeros_like(qv)
    q1_s[...] = jnp.where(lane < DK, qv, zero)
    q2_s[...] = jnp.where(lane >= DK, qv, zero)
    vp_s[0:seq, :] = v_ref[...]
    vp_s[seq:tp, :] = jnp.zeros((tp - seq, DV), F32)
    for j in range(tp // LANES):
        vt_s[:, j * LANES:(j + 1) * LANES] = vp_s[j * LANES:(j + 1) * LANES, :].T.astype(BF16)

    gcol = g_ref[...] * (1.0 - lam_init)

    def attend(q_lo, n_q, kend, b_lo, out_lo, out_rows):
        outs = []
        for m, qs in enumerate((q1_s, q2_s)):
            qt = qs[q_lo:q_lo + n_q, :]
            s = lax.dot_general(kb_s[0:kend, :], qt, (((1,), (1,)), ((), ())),
                                preferred_element_type=F32)
            s = s + bt_ref[b_lo:b_lo + kend, 0:n_q]
            mx = jnp.max(s, axis=0, keepdims=True)
            s_s[m, 0:kend, 0:n_q] = s
            p = jnp.exp(s_s[m, 0:kend, 0:n_q] - mx)
            den = jnp.sum(p, axis=0, keepdims=True)
            p_s[m, 0:kend, 0:n_q] = p.astype(BF16)
            o = jnp.dot(vt_s[:, 0:kend], p_s[m, 0:kend, 0:n_q], preferred_element_type=F32)
            outs.append(o * (1.0 / den))
        ot = outs[0] - lam * outs[1]
        ms = jnp.mean(ot * ot, axis=0, keepdims=True)
        on = (ot * lax.rsqrt(ms + EPS)) * gcol
        o_rows = on.T[0:out_rows, :]
        gb = gb_ref[out_lo:out_lo + out_rows, :]
        o_ref[out_lo:out_lo + out_rows, :] = (o_rows * (gb * _sigmoid(gb))).astype(o_ref.dtype)

    attend(0, LANES, LANES, tp, 0, N_META)
    for i in range(n_tiles):
        q_lo = N_META + Q_TILE * i
        attend(q_lo, Q_TILE, LANES + Q_TILE * (i + 1), Q_TILE * (n_tiles - 1 - i), q_lo, Q_TILE)


def _attn_prompt(q, kstack, vstack, gb, slab, lam_vec, subln_col, layer, lam_init, batch, seq):
    m, width = q.shape
    n_tiles = (seq - N_META) // Q_TILE
    assert N_META + n_tiles * Q_TILE == seq
    tp = (2 * n_tiles + 1) * LANES
    depth = kstack.shape[0]
    q3 = q.reshape(batch, seq, width)
    gb3 = gb.reshape(batch, seq, width)
    k4 = kstack.reshape(depth, batch, seq, width)
    v4 = vstack.reshape(depth, batch, seq, width)
    head = pl.BlockSpec((None, seq, DV), lambda b, h: (b, 0, h))
    kv = pl.BlockSpec((None, None, seq, DV), lambda b, h: (layer, b, 0, h))
    out = pl.pallas_call(
        functools.partial(_attn_prompt_kernel, seq=seq, n_tiles=n_tiles, lam_init=lam_init),
        grid=(batch, N_HEADS_B),
        in_specs=[pl.BlockSpec((None, 4, DK), lambda b, h: (layer, 0, 0)),
                  head, kv, kv, head,
                  pl.BlockSpec((None, tp + LANES, Q_TILE), lambda b, h: (h, 0, 0)),
                  pl.BlockSpec((None, DV, 1), lambda b, h: (layer, 0, 0))],
        out_specs=head,
        out_shape=jax.ShapeDtypeStruct((batch, seq, width), BF16),
        scratch_shapes=[pltpu.VMEM((tp, DV), BF16),
                        pltpu.VMEM((seq, DV), BF16),
                        pltpu.VMEM((seq, DV), BF16),
                        pltpu.VMEM((tp, DV), F32),
                        pltpu.VMEM((DV, tp), BF16),
                        pltpu.VMEM((2, tp, Q_TILE), F32),
                        pltpu.VMEM((2, tp, Q_TILE), BF16)],
        compiler_params=pltpu.CompilerParams(
            dimension_semantics=("arbitrary", "arbitrary"), vmem_limit_bytes=VMEM_LIMIT_BYTES),
        name="diff_attention_prompt",
    )(lam_vec, q3, k4, v4, gb3, slab, subln_col)
    return out.reshape(m, width)


def _attn_sample_kernel(lv_ref, q_ref, kc_ref, vc_ref, kn_ref, vn_ref, gb_ref, bs_ref, g_ref, o_ref,
                        kb_s, vb_s, *, past, dec_seq, lam_init):
    lam = _diff_lambda(lv_ref[...], lam_init)
    width = past + LANES
    kb_s[0:past, :] = kc_ref[...].astype(BF16)
    kb_s[past:past + dec_seq, :] = kn_ref[...].astype(BF16)
    kb_s[past + dec_seq:width, :] = jnp.zeros((LANES - dec_seq, DV), BF16)
    vb_s[0:past, :] = vc_ref[...].astype(BF16)
    vb_s[past:past + dec_seq, :] = vn_ref[...].astype(BF16)
    vb_s[past + dec_seq:width, :] = jnp.zeros((LANES - dec_seq, DV), BF16)

    qv = q_ref[...]
    lane = lax.broadcasted_iota(jnp.int32, qv.shape, 1)
    zero = jnp.zeros_like(qv)
    probs = []
    for qm in (jnp.where(lane < DK, qv, zero), jnp.where(lane >= DK, qv, zero)):
        s = lax.dot_general(qm, kb_s[...], (((1,), (1,)), ((), ())), preferred_element_type=F32)
        s = s + bs_ref[...]
        mx = jnp.max(s, axis=-1, keepdims=True)
        p = jnp.exp(s - mx)
        probs.append(p * (1.0 / jnp.sum(p, axis=-1, keepdims=True)))
    pd = (probs[0] - lam * probs[1]).astype(BF16)
    o = jnp.dot(pd, vb_s[...], preferred_element_type=F32)
    ms = jnp.mean(o * o, axis=-1, keepdims=True)
    on = ((o * lax.rsqrt(ms + EPS)) * g_ref[...]) * (1.0 - lam_init)
    gb = gb_ref[...]
    o_ref[...] = (on * (gb * _sigmoid(gb))).astype(o_ref.dtype)


def _attn_sample(q, kstack, vstack, gb, cache_k, cache_v, slab, lam_vec, subln_row, layer, lam_init,
                 batch, dec_seq):
    m, width = q.shape
    depth, _, past = cache_k.shape[:3]
    ck = cache_k.reshape(depth, batch, past, width)
    cv = cache_v.reshape(depth, batch, past, width)
    head = pl.BlockSpec((dec_seq, DV), lambda b, h: (b, h))
    new = pl.BlockSpec((None, dec_seq, DV), lambda b, h: (layer, b, h))
    cache = pl.BlockSpec((None, None, past, DV), lambda b, h: (layer, b, 0, h))
    return pl.pallas_call(
        functools.partial(_attn_sample_kernel, past=past, dec_seq=dec_seq, lam_init=lam_init),
        grid=(batch, N_HEADS_B),
        in_specs=[pl.BlockSpec((None, 4, DK), lambda b, h: (layer, 0, 0)),
                  head, cache, cache, new, new, head,
                  pl.BlockSpec((None, dec_seq, past + LANES), lambda b, h: (h, 0, 0)),
                  pl.BlockSpec((None, 1, DV), lambda b, h: (layer, 0, 0))],
        out_specs=head,
        out_shape=jax.ShapeDtypeStruct((m, width), BF16),
        scratch_shapes=[pltpu.VMEM((past + LANES, DV), BF16),
                        pltpu.VMEM((past + LANES, DV), BF16)],
        compiler_params=pltpu.CompilerParams(
            dimension_semantics=("arbitrary", "arbitrary"), vmem_limit_bytes=VMEM_LIMIT_BYTES),
        name="diff_attention_sample",
    )(lam_vec, q, ck, cv, kstack, vstack, gb, slab, subln_row)


def _outproj_kernel(ya_ref, yb_ref, x_ref, w_ref, g_ref, o_ref, y_s, *, col_chunk):
    d_a = ya_ref.shape[-1]
    d = o_ref.shape[-1]
    for c in range(0, d, col_chunk):
        y_s[:, c:c + col_chunk] = (
            jnp.dot(ya_ref[...], w_ref[0:d_a, c:c + col_chunk], preferred_element_type=F32)
            + jnp.dot(yb_ref[...], w_ref[d_a:, c:c + col_chunk], preferred_element_type=F32))
    y = y_s[...]
    ms = jnp.mean(y * y, axis=-1, keepdims=True)
    o_ref[...] = x_ref[...] + (y * lax.rsqrt(ms + EPS)) * g_ref[...]


def _outproj(ya, yb, x2d, w_out_b, post_g, layer):
    m, d = x2d.shape
    tm = _row_tile(m, 384)
    half = pl.BlockSpec((tm, d // 2), lambda i: (i, 0))
    full = pl.BlockSpec((tm, d), lambda i: (i, 0))
    return pl.pallas_call(
        functools.partial(_outproj_kernel, col_chunk=512),
        grid=(m // tm,),
        in_specs=[half, half, full,
                  pl.BlockSpec((None, d, d), lambda i: (layer, 0, 0)),
                  pl.BlockSpec((None, 1, d), lambda i: (layer, 0, 0))],
        out_specs=full,
        out_shape=jax.ShapeDtypeStruct((m, d), F32),
        scratch_shapes=[pltpu.VMEM((tm, d), F32)],
        compiler_params=pltpu.CompilerParams(
            dimension_semantics=("arbitrary",), vmem_limit_bytes=VMEM_LIMIT_BYTES),
        name="out_projection",
    )(ya, yb, x2d, w_out_b, post_g)


def _block_diag_gates(w):
    depth, nb, bs, _ = w.shape
    per = GATE_BLOCK // bs
    w5 = w.reshape(depth, nb // per, per, bs, bs)
    eye = jnp.eye(per, dtype=w.dtype)
    bd = jnp.einsum('lgaij,ab->lgaibj', w5, eye)
    return bd.reshape(depth, nb // per, GATE_BLOCK, GATE_BLOCK).astype(BF16)


def kernel(x_prompt, x_sample, cache_k, cache_v, state_conv, state_rglru, meta, rel_bias, pre_g, post_g,
           w_in, conv_w, conv_b, gate_r_w, gate_r_b, gate_i_w, gate_i_b, rglru_lam, lam_q1, lam_k1,
           lam_q2, lam_k2, subln_g, w_out):
    batch, seq0, d = x_prompt.shape
    dec_batch, dec_seq, _ = x_sample.shape
    depth = w_in.shape[0]
    past = cache_k.shape[2]
    d_a = d // 2
    seq = seq0 + N_META
    n_tiles = seq0 // Q_TILE

    hp = jnp.concatenate(
        [jnp.broadcast_to(meta.astype(x_prompt.dtype)[None], (batch, N_META, d)), x_prompt],
        axis=1).reshape(batch * seq, d)
    hs = x_sample.reshape(dec_batch * dec_seq, d)

    w_in_b = w_in.astype(BF16)
    w_out_b = w_out.astype(BF16)
    wr_bd = _block_diag_gates(gate_r_w)
    wi_bd = _block_diag_gates(gate_i_w)
    pre_g3 = pre_g.reshape(depth, 1, d)
    post_g3 = post_g.reshape(depth, 1, d)
    conv_b3 = conv_b.reshape(depth, 1, d_a)
    br3 = gate_r_b.reshape(depth, 1, d_a)
    bi3 = gate_i_b.reshape(depth, 1, d_a)
    lam3 = rglru_lam.reshape(depth, 1, d_a)
    lam_vec = jnp.stack([lam_q1, lam_k1, lam_q2, lam_k2], axis=1)
    subln_col = subln_g.reshape(depth, DV, 1)
    subln_row = subln_g.reshape(depth, 1, DV)

    slab_p, slab_s = _bias_slabs(rel_bias, n_tiles, past, dec_seq)

    zero_buf = jnp.zeros((batch, SUBLANES, d_a), F32)
    zero_h = jnp.zeros((batch, 1, d_a), F32)
    pad_rows = SUBLANES - (CONV_W - 1)
    state_conv8 = jnp.pad(state_conv, ((0, 0), (0, 0), (pad_rows, 0), (0, 0)))

    kv_p = None
    kv_s = None
    cp_l, rp_l, cs_l, rs_l = [], [], [], []
    for l in range(depth):
        lam_init = 0.8 - 0.6 * math.exp(-0.3 * l)
        xa, ga, q, kst, vst, gb = _inproj(hp, pre_g3, w_in_b, l, kv_p, depth)
        kv_p = (kst, vst)
        ya, cp, rp = _rglru(xa, ga, zero_buf, zero_h, conv_w, conv_b3, wr_bd, br3, wi_bd, bi3, lam3,
                            l, batch, seq)
        yb = _attn_prompt(q, kst, vst, gb, slab_p, lam_vec, subln_col, l, lam_init, batch, seq)
        hp = _outproj(ya, yb, hp, w_out_b, post_g3, l)
        cp_l.append(cp)
        rp_l.append(rp)
        xa, ga, q, kst, vst, gb = _inproj(hs, pre_g3, w_in_b, l, kv_s, depth)
        kv_s = (kst, vst)
        ya, cs, rs = _rglru(xa, ga, state_conv8[l], state_rglru[l][:, None, :], conv_w, conv_b3,
                            wr_bd, br3, wi_bd, bi3, lam3, l, dec_batch, dec_seq)
        yb = _attn_sample(q, kst, vst, gb, cache_k, cache_v, slab_s, lam_vec, subln_row, l, lam_init,
                          dec_batch, dec_seq)
        hs = _outproj(ya, yb, hs, w_out_b, post_g3, l)
        cs_l.append(cs)
        rs_l.append(rs)

    y_prompt = hp.reshape(batch, seq, d)[:, N_META:]
    y_sample = hs.reshape(dec_batch, dec_seq, d)
    k_prompt = kv_p[0].reshape(depth, batch, seq, N_HEADS_B, 2 * DK)
    v_prompt = kv_p[1].reshape(depth, batch, seq, N_HEADS_B, DV)
    k_sample = kv_s[0].reshape(depth, dec_batch, dec_seq, N_HEADS_B, 2 * DK)
    v_sample = kv_s[1].reshape(depth, dec_batch, dec_seq, N_HEADS_B, DV)
    return (y_prompt, y_sample, k_prompt, v_prompt, jnp.stack(cp_l), jnp.stack(rp_l),
            k_sample, v_sample, jnp.stack(cs_l), jnp.stack(rs_l))
```

```python
import functools
import math

import jax
import jax.numpy as jnp
from jax import lax
from jax.experimental import pallas as pl
from jax.experimental.pallas import tpu as pltpu

F32 = jnp.float32
BF16 = jnp.bfloat16

CHUNK = 64
N_META = 16
N_BLOCKS_A = 16
CONV_W = 4
RG_C = 8.0
N_HEADS_B = 8
DK = 64
DV = 2 * DK
NUM_BUCKETS = 32
REL_MAX_DIST = 1024
EPS = 1e-6

LANES = 128
SUBLANES = 8
BF16_ROWS = 16
VMEM_LIMIT_BYTES = 56 * 1024 * 1024

LOG2E = math.log2(math.e)
Q_TILE = 256
GATE_BLOCK = 256


def _sigmoid(x):
    return 1.0 / (1.0 + jnp.exp(-x))


def _row_tile(m, cap=512):
    best = None
    for t in range(BF16_ROWS, min(m, cap) + 1, BF16_ROWS):
        if m % t == 0:
            best = t
    assert best is not None, m
    return best


def _bias_lookup(rel, tab_ref, h):
    half = NUM_BUCKETS // 2
    max_exact = half // 2
    ret = jnp.where(rel > 0, half, 0).astype(jnp.int32)
    n = jnp.abs(rel)
    nf = jnp.maximum(n, 1).astype(F32)
    large = max_exact + (jnp.log(nf / max_exact) / math.log(REL_MAX_DIST / max_exact)
                         * (half - max_exact)).astype(jnp.int32)
    large = jnp.minimum(large, half - 1)
    bucket = ret + jnp.where(n < max_exact, n, large)
    val = jnp.zeros(rel.shape, F32)
    for b in range(NUM_BUCKETS):
        val = jnp.where(bucket == b, tab_ref[b, h], val)
    return val


def _prompt_slab_kernel(tab_ref, out_ref, *, n_tiles):
    h = pl.program_id(0)
    seq = Q_TILE * n_tiles + N_META
    diag0 = Q_TILE * (n_tiles - 1) + N_META
    r0 = 0
    while r0 < seq:
        blk = min(LANES, seq - r0)
        r = lax.broadcasted_iota(jnp.int32, (blk, Q_TILE), 0) + r0
        c = lax.broadcasted_iota(jnp.int32, (blk, Q_TILE), 1)
        val = _bias_lookup(r - c - diag0, tab_ref, h) * LOG2E
        rp = r - diag0
        masked = jnp.logical_and(rp >= 0, (rp // CHUNK) > (c // CHUNK))
        out_ref[r0:r0 + blk, :] = jnp.where(masked, -jnp.inf, val)
        r0 += blk
    k = lax.broadcasted_iota(jnp.int32, (N_META, Q_TILE), 0)
    c = lax.broadcasted_iota(jnp.int32, (N_META, Q_TILE), 1)
    out_ref[seq:seq + N_META, :] = _bias_lookup(k - c, tab_ref, h) * LOG2E


def _sample_slab_kernel(tab_ref, out_ref, *, past, dec_seq):
    h = pl.program_id(0)
    width = out_ref.shape[-1]
    for c0 in range(0, width, 512):
        w = min(512, width - c0)
        t = lax.broadcasted_iota(jnp.int32, (dec_seq, w), 0)
        k = lax.broadcasted_iota(jnp.int32, (dec_seq, w), 1) + c0
        val = _bias_lookup(k - past - t, tab_ref, h) * LOG2E
        out_ref[:, c0:c0 + w] = jnp.where(k >= past + dec_seq, -jnp.inf, val)


def _bias_slabs(rel_bias, n_tiles, past, dec_seq):
    rows = Q_TILE * n_tiles + 2 * N_META
    smem = pl.BlockSpec(memory_space=pltpu.SMEM)
    prompt = pl.pallas_call(
        functools.partial(_prompt_slab_kernel, n_tiles=n_tiles),
        grid=(N_HEADS_B,),
        in_specs=[smem],
        out_specs=pl.BlockSpec((None, rows, Q_TILE), lambda h: (h, 0, 0)),
        out_shape=jax.ShapeDtypeStruct((N_HEADS_B, rows, Q_TILE), F32),
        name="prompt_bias_slab",
    )(rel_bias)
    width = past + LANES
    sample = pl.pallas_call(
        functools.partial(_sample_slab_kernel, past=past, dec_seq=dec_seq),
        grid=(N_HEADS_B,),
        in_specs=[smem],
        out_specs=pl.BlockSpec((None, dec_seq, width), lambda h: (h, 0, 0)),
        out_shape=jax.ShapeDtypeStruct((N_HEADS_B, dec_seq, width), F32),
        name="sample_bias_slab",
    )(rel_bias)
    return prompt, sample


def _inproj_kernel(x_ref, g_ref, w_ref, *rest, col_chunk):
    xa_ref, ga_ref, q_ref, kb_ref, vb_ref, gb_ref, k5_ref, v5_ref, u_s = rest[-9:]
    tm = x_ref.shape[0]
    x = x_ref[...]
    ms = jnp.mean(x * x, axis=-1, keepdims=True)
    u_s[...] = ((x * lax.rsqrt(ms + EPS)) * g_ref[...]).astype(BF16)
    col = 0
    for ref, scale, ref5 in ((xa_ref, None, None), (ga_ref, None, None),
                             (q_ref, (DK ** -0.5) * LOG2E, None),
                             (kb_ref, None, k5_ref), (vb_ref, None, v5_ref), (gb_ref, None, None)):
        width = ref.shape[-1]
        for c in range(0, width, col_chunk):
            acc = jnp.dot(u_s[...], w_ref[:, col + c:col + c + col_chunk], preferred_element_type=F32)
            if scale is not None:
                acc = acc * scale
            ref[:, c:c + col_chunk] = acc.astype(ref.dtype)
            if ref5 is not None:
                for j in range(col_chunk // DV):
                    head = c // DV + j
                    ref5[pl.ds(head, tm, stride=N_HEADS_B), :] = acc[:, j * DV:(j + 1) * DV]
        col += width


def _inproj(x2d, pre_g, w_in_b, layer, kv_stacks, depth):
    m, d = x2d.shape
    d_a = d // 2
    width = N_HEADS_B * DV
    in_cols = w_in_b.shape[-1]
    tm = _row_tile(m, 384)
    row = lambda w: pl.BlockSpec((tm, w), lambda i: (i, 0))
    stack = pl.BlockSpec((None, tm * N_HEADS_B, DV), lambda i: (layer, i, 0))
    in_specs = [
        row(d),
        pl.BlockSpec((None, 1, d), lambda i: (layer, 0, 0)),
        pl.BlockSpec((None, d, in_cols), lambda i: (layer, 0, 0), pipeline_mode=pl.Buffered(1)),
    ]
    args = [x2d, pre_g, w_in_b]
    aliases = {}
    if kv_stacks is not None:
        in_specs += [pl.BlockSpec(memory_space=pl.ANY)] * 2
        args += list(kv_stacks)
        aliases = {3: 6, 4: 7}
    out_shape = [
        jax.ShapeDtypeStruct((m, d_a), F32),
        jax.ShapeDtypeStruct((m, d_a), F32),
        jax.ShapeDtypeStruct((m, width), BF16),
        jax.ShapeDtypeStruct((m, width), BF16),
        jax.ShapeDtypeStruct((m, width), BF16),
        jax.ShapeDtypeStruct((m, width), BF16),
        jax.ShapeDtypeStruct((depth, m * N_HEADS_B, DV), F32),
        jax.ShapeDtypeStruct((depth, m * N_HEADS_B, DV), F32),
    ]
    out_specs = [row(d_a), row(d_a), row(width), row(width), row(width), row(width), stack, stack]
    return pl.pallas_call(
        functools.partial(_inproj_kernel, col_chunk=512),
        grid=(m // tm,),
        in_specs=in_specs,
        out_specs=out_specs,
        out_shape=out_shape,
        scratch_shapes=[pltpu.VMEM((tm, d), BF16)],
        input_output_aliases=aliases,
        compiler_params=pltpu.CompilerParams(
            dimension_semantics=("arbitrary",), vmem_limit_bytes=VMEM_LIMIT_BYTES),
        name="in_projection",
    )(*args)


def _rglru_kernel(xa_ref, ga_ref, buf_ref, h0_ref, cw_ref, cb_ref, wr_ref, br_ref, wi_ref, bi_ref,
                  lam_ref, y_ref, nbuf_ref, hl_ref, xp_s, a_s, b_s, hs_s, as_s, hn_s, *, seq, seg, g_block):
    c = xa_ref.shape[-1]
    nl = c // LANES
    lanes = lambda j: slice(j * LANES, (j + 1) * LANES)
    pad = SUBLANES * seg - seq
    for j in range(nl):
        xp_s[j, 0:SUBLANES, :] = buf_ref[:, lanes(j)]
        xp_s[j, SUBLANES:SUBLANES + seq, :] = xa_ref[:, lanes(j)]
        if pad:
            xp_s[j, SUBLANES + seq:, :] = jnp.zeros((pad, LANES), F32)
        nbuf_ref[:, lanes(j)] = xp_s[j, seq:seq + SUBLANES, :]

    neg = -lam_ref[...]
    softplus = jnp.maximum(neg, 0.0) + jnp.log1p(jnp.exp(-jnp.abs(neg)))
    cneg = -RG_C * softplus
    cb = cb_ref[...]
    taps = [cw_ref[j:j + 1, :] for j in range(CONV_W)]

    cache = {}

    def rows_at(g):
        if g not in cache:
            cache[g] = jnp.concatenate(
                [xp_s[j, pl.ds(SUBLANES + g, SUBLANES, stride=seg), :] for j in range(nl)], axis=1)
        return cache[g]

    for g0 in range(0, seg, g_block):
        pieces = []
        for g in range(g0, g0 + g_block):
            xc = cb
            for j in range(CONV_W):
                xc = xc + rows_at(g - (CONV_W - 1) + j) * taps[j]
            pieces.append(xc)
        xc = jnp.concatenate(pieces, axis=0)
        xcb = xc.astype(BF16)
        r = _sigmoid(jnp.dot(xcb, wr_ref[...], preferred_element_type=F32) + br_ref[...])
        ig = _sigmoid(jnp.dot(xcb, wi_ref[...], preferred_element_type=F32) + bi_ref[...])
        log_a = r * cneg
        a = jnp.exp(log_a)
        lo, hi = g0 * SUBLANES, (g0 + g_block) * SUBLANES
        a_s[lo:hi, :] = a
        e = -jnp.tanh(log_a) * (a * a + 1.0)
        root = jnp.where(e > 0.0, e * lax.rsqrt(e), 0.0)
        b_s[lo:hi, :] = root * (ig * xc)

    rowid = lax.broadcasted_iota(jnp.int32, (SUBLANES, c), 0)
    first = rowid == 0

    def scan_body(g, carry):
        h, acc = carry
        r = pl.multiple_of(g * SUBLANES, SUBLANES)
        a = a_s[pl.ds(r, SUBLANES), :]
        h = a * h + b_s[pl.ds(r, SUBLANES), :]
        acc = a * acc
        hs_s[pl.ds(r, SUBLANES), :] = h
        as_s[pl.ds(r, SUBLANES), :] = acc
        return h, acc

    h_init = jnp.where(first, jnp.broadcast_to(h0_ref[...], (SUBLANES, c)), 0.0)
    end_h, end_a = lax.fori_loop(0, seg, scan_body, (h_init, jnp.ones((SUBLANES, c), F32)), unroll=8)

    d = jnp.zeros((SUBLANES, c), F32)
    for _ in range(SUBLANES - 1):
        d = jnp.where(first, 0.0, pltpu.roll(end_h + end_a * d, 1, 0))

    def fix_body(g, carry):
        r = pl.multiple_of(g * SUBLANES, SUBLANES)
        h = hs_s[pl.ds(r, SUBLANES), :] + as_s[pl.ds(r, SUBLANES), :] * d
        for j in range(nl):
            hn_s[j, pl.ds(g, SUBLANES, stride=seg), :] = h[:, lanes(j)]
        return carry

    lax.fori_loop(0, seg, fix_body, 0, unroll=8)

    for j in range(nl):
        ga = ga_ref[:, lanes(j)]
        y_ref[:, lanes(j)] = (hn_s[j, 0:seq, :] * (ga * _sigmoid(ga))).astype(y_ref.dtype)
        hl_ref[:, lanes(j)] = hn_s[j, seq - 1:seq, :]


def _rglru(xa, ga, buf8, h0, conv_w, conv_b, wr_bd, br, wi_bd, bi, lam, layer, batch, seq):
    m, d_a = xa.shape
    xa3 = xa.reshape(batch, seq, d_a)
    ga3 = ga.reshape(batch, seq, d_a)
    nc = d_a // GATE_BLOCK
    nl = GATE_BLOCK // LANES
    seg = pl.cdiv(pl.cdiv(seq, SUBLANES), SUBLANES) * SUBLANES
    g_block = max(t for t in range(1, min(seg, 44) + 1) if seg % t == 0)
    chan = lambda rows: pl.BlockSpec((None, rows, GATE_BLOCK), lambda b, c: (b, 0, c))
    par = lambda rows: pl.BlockSpec((None, rows, GATE_BLOCK), lambda b, c: (layer, 0, c))
    wspec = pl.BlockSpec((None, None, GATE_BLOCK, GATE_BLOCK), lambda b, c: (layer, c, 0, 0))
    y, nbuf, hl = pl.pallas_call(
        functools.partial(_rglru_kernel, seq=seq, seg=seg, g_block=g_block),
        grid=(batch, nc),
        in_specs=[chan(seq), chan(seq), chan(SUBLANES), chan(1), par(CONV_W), par(1),
                  wspec, par(1), wspec, par(1), par(1)],
        out_specs=[chan(seq), chan(SUBLANES), chan(1)],
        out_shape=[jax.ShapeDtypeStruct((batch, seq, d_a), BF16),
                   jax.ShapeDtypeStruct((batch, SUBLANES, d_a), F32),
                   jax.ShapeDtypeStruct((batch, 1, d_a), F32)],
        scratch_shapes=[pltpu.VMEM((nl, SUBLANES * (seg + 1), LANES), F32),
                        pltpu.VMEM((SUBLANES * seg, GATE_BLOCK), F32),
                        pltpu.VMEM((SUBLANES * seg, GATE_BLOCK), F32),
                        pltpu.VMEM((SUBLANES * seg, GATE_BLOCK), F32),
                        pltpu.VMEM((SUBLANES * seg, GATE_BLOCK), F32),
                        pltpu.VMEM((nl, SUBLANES * seg, LANES), F32)],
        compiler_params=pltpu.CompilerParams(
            dimension_semantics=("arbitrary", "arbitrary"), vmem_limit_bytes=VMEM_LIMIT_BYTES),
        name="rglru_mixer",
    )(xa3, ga3, buf8, h0, conv_w, conv_b, wr_bd, br, wi_bd, bi, lam)
    return y.reshape(m, d_a), nbuf[:, SUBLANES - (CONV_W - 1):, :], hl[:, 0, :]


def _diff_lambda(lv, lam_init):
    s1 = jnp.sum(lv[0:1, :] * lv[1:2, :], axis=-1, keepdims=True)
    s2 = jnp.sum(lv[2:3, :] * lv[3:4, :], axis=-1, keepdims=True)
    return jnp.exp(s1) - jnp.exp(s2) + lam_init


def _attn_prompt_kernel(lv_ref, q_ref, k_ref, v_ref, gb_ref, bt_ref, g_ref, o_ref,
                        q1t_s, q2t_s, vt_s, acc_s, s_s, *, seq, n_tiles, lam_init):
    gap = LANES - N_META
    lam = _diff_lambda(lv_ref[...], lam_init)

    row = lax.broadcasted_iota(jnp.int32, (2 * DK, LANES), 0)
    zero = jnp.zeros((2 * DK, LANES), BF16)
    pad = jnp.zeros((gap, DV), BF16)
    for j in range(2 * n_tiles + 1):
        cols = slice(LANES * j, LANES * (j + 1))
        if j == 0:
            qb = jnp.concatenate([q_ref[0:N_META, :], pad], axis=0)
            vb = jnp.concatenate([v_ref[0:N_META, :], pad], axis=0)
        else:
            rows = slice(N_META + LANES * (j - 1), N_META + LANES * j)
            qb = q_ref[rows, :]
            vb = v_ref[rows, :]
        qt = qb.T
        q1t_s[:, cols] = jnp.where(row < DK, qt, zero)
        q2t_s[:, cols] = jnp.where(row >= DK, qt, zero)
        vt_s[:, cols] = vb.T

    gcol = g_ref[...] * (1.0 - lam_init)
    meta_pad = jnp.zeros((gap, Q_TILE), BF16)

    tiles = [(0, LANES, 0, seq, 0, N_META)]
    for i in range(n_tiles):
        tiles.append((LANES + Q_TILE * i, Q_TILE, i + 1, Q_TILE * (n_tiles - 1 - i),
                      N_META + Q_TILE * i, Q_TILE))
    q_maps = (q1t_s, q2t_s)
    state = {}

    def key_rows(t, j):
        if j == 0:
            return 0, N_META + (Q_TILE if t else 0)
        return N_META + Q_TILE * j, Q_TILE

    def scores(t, j, slot):
        q_col, n_q, _, b_lo, _, _ = tiles[t]
        r0, nr = key_rows(t, j)
        cms = []
        for m in range(2):
            s = jnp.dot(k_ref[r0:r0 + nr, :], q_maps[m][:, q_col:q_col + n_q],
                        preferred_element_type=F32)
            s = s + bt_ref[b_lo + r0:b_lo + r0 + nr, 0:n_q]
            s_s[slot, m, 0:nr, 0:n_q] = s
            cms.append(jnp.max(s, axis=0, keepdims=True))
        return cms

    def fold(t, j, slot, cms):
        _, n_q, _, _, _, _ = tiles[t]
        r0, nr = key_rows(t, j)
        ops = []
        for m in range(2):
            if j == 0:
                m_new, alpha = cms[m], None
                s = s_s[slot, m, 0:nr, 0:n_q]
                p = jnp.exp2(s - m_new)
                den = jnp.sum(p, axis=0, keepdims=True)
                pb = p.astype(BF16)
                parts = [pb[0:N_META], meta_pad[:, 0:n_q]]
                if nr > N_META:
                    parts.append(pb[N_META:nr])
                pb = jnp.concatenate(parts, axis=0)
                c0, nc = 0, LANES + nr - N_META
            else:
                mx, den = state[t, m]
                m_new = jnp.maximum(mx, cms[m])
                alpha = jnp.exp2(mx - m_new)
                p = jnp.exp2(s_s[slot, m, 0:nr, 0:n_q] - m_new)
                den = alpha * den + jnp.sum(p, axis=0, keepdims=True)
                pb = p.astype(BF16)
                c0, nc = LANES + Q_TILE * j, Q_TILE
            state[t, m] = (m_new, den)
            ops.append((alpha, pb, c0, nc))
        for m, (alpha, pb, c0, nc) in enumerate(ops):
            o = jnp.dot(vt_s[:, c0:c0 + nc], pb, preferred_element_type=F32)
            if alpha is None:
                acc_s[t, m, :, 0:n_q] = o
            else:
                acc_s[t, m, :, 0:n_q] = alpha * acc_s[t, m, :, 0:n_q] + o

    def finish(t):
        _, n_q, _, _, q_lo, out_rows = tiles[t]
        outs = [acc_s[t, m, :, 0:n_q] * (1.0 / state[t, m][1]) for m in range(2)]
        ot = outs[0] - lam * outs[1]
        ms = jnp.mean(ot * ot, axis=0, keepdims=True)
        on = (ot * lax.rsqrt(ms + EPS)) * gcol
        o_rows = on.T[0:out_rows, :]
        gb = gb_ref[q_lo:q_lo + out_rows, :].astype(F32)
        o_ref[q_lo:q_lo + out_rows, :] = (o_rows * (gb * _sigmoid(gb))).astype(o_ref.dtype)

    jobs = [(0, 0)] + [(t, j) for _, t, j in sorted(
        ((j + 0.5) / tiles[t][2], t, j) for t in range(1, len(tiles)) for j in range(tiles[t][2]))]
    cms_next = scores(*jobs[0], 0)
    for n, (t, j) in enumerate(jobs):
        cms = cms_next
        if n + 1 < len(jobs):
            cms_next = scores(*jobs[n + 1], (n + 1) % 2)
        fold(t, j, n % 2, cms)
        if j == max(tiles[t][2] - 1, 0):
            finish(t)


def _attn_prompt(q, kb, vb, gb, slab, lam_vec, subln_col, layer, lam_init, batch, seq):
    m, width = q.shape
    n_tiles = (seq - N_META) // Q_TILE
    assert N_META + n_tiles * Q_TILE == seq
    tp = (2 * n_tiles + 1) * LANES
    to3 = lambda a: a.reshape(batch, seq, width)
    head = pl.BlockSpec((None, seq, DV), lambda b, h: (b, 0, h))
    out = pl.pallas_call(
        functools.partial(_attn_prompt_kernel, seq=seq, n_tiles=n_tiles, lam_init=lam_init),
        grid=(batch, N_HEADS_B),
        in_specs=[pl.BlockSpec((None, 4, DK), lambda b, h: (layer, 0, 0)),
                  head, head, head, head,
                  pl.BlockSpec((None, seq + N_META, Q_TILE), lambda b, h: (h, 0, 0)),
                  pl.BlockSpec((None, DV, 1), lambda b, h: (layer, 0, 0))],
        out_specs=head,
        out_shape=jax.ShapeDtypeStruct((batch, seq, width), BF16),
        scratch_shapes=[pltpu.VMEM((2 * DK, tp), BF16),
                        pltpu.VMEM((2 * DK, tp), BF16),
                        pltpu.VMEM((DV, tp), BF16),
                        pltpu.VMEM((n_tiles + 1, 2, DV, Q_TILE), F32),
                        pltpu.VMEM((2, 2, N_META + Q_TILE, Q_TILE), F32)],
        compiler_params=pltpu.CompilerParams(
            dimension_semantics=("arbitrary", "arbitrary"), vmem_limit_bytes=VMEM_LIMIT_BYTES),
        name="diff_attention_prompt",
    )(lam_vec, to3(q), to3(kb), to3(vb), to3(gb), slab, subln_col)
    return out.reshape(m, width)


def _attn_sample_kernel(lv_ref, q_ref, kc_ref, vc_ref, kn_ref, vn_ref, gb_ref, bs_ref, g_ref, o_ref,
                        kb_s, vb_s, *, past, dec_seq, lam_init):
    lam = _diff_lambda(lv_ref[...], lam_init)
    width = past + LANES
    kb_s[0:past, :] = kc_ref[...].astype(BF16)
    kb_s[past:past + dec_seq, :] = kn_ref[...]
    kb_s[past + dec_seq:width, :] = jnp.zeros((LANES - dec_seq, DV), BF16)
    vb_s[0:past, :] = vc_ref[...].astype(BF16)
    vb_s[past:past + dec_seq, :] = vn_ref[...]
    vb_s[past + dec_seq:width, :] = jnp.zeros((LANES - dec_seq, DV), BF16)

    qv = q_ref[...]
    lane = lax.broadcasted_iota(jnp.int32, qv.shape, 1)
    zero = jnp.zeros_like(qv)
    probs = []
    for qm in (jnp.where(lane < DK, qv, zero), jnp.where(lane >= DK, qv, zero)):
        s = lax.dot_general(qm, kb_s[...], (((1,), (1,)), ((), ())), preferred_element_type=F32)
        s = s + bs_ref[...]
        mx = jnp.max(s, axis=-1, keepdims=True)
        p = jnp.exp2(s - mx)
        probs.append(p * (1.0 / jnp.sum(p, axis=-1, keepdims=True)))
    pd = (probs[0] - lam * probs[1]).astype(BF16)
    o = jnp.dot(pd, vb_s[...], preferred_element_type=F32)
    ms = jnp.mean(o * o, axis=-1, keepdims=True)
    on = ((o * lax.rsqrt(ms + EPS)) * g_ref[...]) * (1.0 - lam_init)
    gb = gb_ref[...].astype(F32)
    o_ref[...] = (on * (gb * _sigmoid(gb))).astype(o_ref.dtype)


def _attn_sample(q, kb, vb, gb, cache_k, cache_v, slab, lam_vec, subln_row, layer, lam_init,
                 batch, dec_seq):
    m, width = q.shape
    depth, _, past = cache_k.shape[:3]
    ck = cache_k.reshape(depth, batch, past, width)
    cv = cache_v.reshape(depth, batch, past, width)
    head = pl.BlockSpec((dec_seq, DV), lambda b, h: (b, h))
    cache = pl.BlockSpec((None, None, past, DV), lambda b, h: (layer, b, 0, h))
    return pl.pallas_call(
        functools.partial(_attn_sample_kernel, past=past, dec_seq=dec_seq, lam_init=lam_init),
        grid=(batch, N_HEADS_B),
        in_specs=[pl.BlockSpec((None, 4, DK), lambda b, h: (layer, 0, 0)),
                  head, cache, cache, head, head, head,
                  pl.BlockSpec((None, dec_seq, past + LANES), lambda b, h: (h, 0, 0)),
                  pl.BlockSpec((None, 1, DV), lambda b, h: (layer, 0, 0))],
        out_specs=head,
        out_shape=jax.ShapeDtypeStruct((m, width), BF16),
        scratch_shapes=[pltpu.VMEM((past + LANES, DV), BF16),
                        pltpu.VMEM((past + LANES, DV), BF16)],
        compiler_params=pltpu.CompilerParams(
            dimension_semantics=("arbitrary", "arbitrary"), vmem_limit_bytes=VMEM_LIMIT_BYTES),
        name="diff_attention_sample",
    )(lam_vec, q, ck, cv, kb, vb, gb, slab, subln_row)


def _outproj_kernel(ya_ref, yb_ref, x_ref, w_ref, g_ref, o_ref, y_s, *, col_chunk):
    d_a = ya_ref.shape[-1]
    d = o_ref.shape[-1]
    for c in range(0, d, col_chunk):
        y_s[:, c:c + col_chunk] = (
            jnp.dot(ya_ref[...], w_ref[0:d_a, c:c + col_chunk], preferred_element_type=F32)
            + jnp.dot(yb_ref[...], w_ref[d_a:, c:c + col_chunk], preferred_element_type=F32))
    y = y_s[...]
    ms = jnp.mean(y * y, axis=-1, keepdims=True)
    o_ref[...] = x_ref[...] + (y * lax.rsqrt(ms + EPS)) * g_ref[...]


def _outproj(ya, yb, x2d, w_out_b, post_g, layer):
    m, d = x2d.shape
    tm = _row_tile(m, 384)
    half = pl.BlockSpec((tm, d // 2), lambda i: (i, 0))
    full = pl.BlockSpec((tm, d), lambda i: (i, 0))
    return pl.pallas_call(
        functools.partial(_outproj_kernel, col_chunk=512),
        grid=(m // tm,),
        in_specs=[half, half, full,
                  pl.BlockSpec((None, d, d), lambda i: (layer, 0, 0)),
                  pl.BlockSpec((None, 1, d), lambda i: (layer, 0, 0))],
        out_specs=full,
        out_shape=jax.ShapeDtypeStruct((m, d), F32),
        scratch_shapes=[pltpu.VMEM((tm, d), F32)],
        compiler_params=pltpu.CompilerParams(
            dimension_semantics=("arbitrary",), vmem_limit_bytes=VMEM_LIMIT_BYTES),
        name="out_projection",
    )(ya, yb, x2d, w_out_b, post_g)


def _block_diag_gates(w):
    depth, nb, bs, _ = w.shape
    per = GATE_BLOCK // bs
    w5 = w.reshape(depth, nb // per, per, bs, bs)
    eye = jnp.eye(per, dtype=w.dtype)
    bd = jnp.einsum('lgaij,ab->lgaibj', w5, eye)
    return bd.reshape(depth, nb // per, GATE_BLOCK, GATE_BLOCK).astype(BF16)


def kernel(x_prompt, x_sample, cache_k, cache_v, state_conv, state_rglru, meta, rel_bias, pre_g, post_g,
           w_in, conv_w, conv_b, gate_r_w, gate_r_b, gate_i_w, gate_i_b, rglru_lam, lam_q1, lam_k1,
           lam_q2, lam_k2, subln_g, w_out):
    batch, seq0, d = x_prompt.shape
    dec_batch, dec_seq, _ = x_sample.shape
    depth = w_in.shape[0]
    past = cache_k.shape[2]
    d_a = d // 2
    seq = seq0 + N_META
    n_tiles = seq0 // Q_TILE

    hp = jnp.concatenate(
        [jnp.broadcast_to(meta.astype(x_prompt.dtype)[None], (batch, N_META, d)), x_prompt],
        axis=1).reshape(batch * seq, d)
    hs = x_sample.reshape(dec_batch * dec_seq, d)

    w_in_b = w_in.astype(BF16)
    w_out_b = w_out.astype(BF16)
    wr_bd = _block_diag_gates(gate_r_w)
    wi_bd = _block_diag_gates(gate_i_w)
    pre_g3 = pre_g.reshape(depth, 1, d)
    post_g3 = post_g.reshape(depth, 1, d)
    conv_b3 = conv_b.reshape(depth, 1, d_a)
    br3 = gate_r_b.reshape(depth, 1, d_a)
    bi3 = gate_i_b.reshape(depth, 1, d_a)
    lam3 = rglru_lam.reshape(depth, 1, d_a)
    lam_vec = jnp.stack([lam_q1, lam_k1, lam_q2, lam_k2], axis=1)
    subln_col = subln_g.reshape(depth, DV, 1)
    subln_row = subln_g.reshape(depth, 1, DV)

    slab_p, slab_s = _bias_slabs(rel_bias, n_tiles, past, dec_seq)

    zero_buf = jnp.zeros((batch, SUBLANES, d_a), F32)
    zero_h = jnp.zeros((batch, 1, d_a), F32)
    pad_rows = SUBLANES - (CONV_W - 1)
    state_conv8 = jnp.pad(state_conv, ((0, 0), (0, 0), (pad_rows, 0), (0, 0)))

    kv_p = None
    kv_s = None
    cp_l, rp_l, cs_l, rs_l = [], [], [], []
    for l in range(depth):
        lam_init = 0.8 - 0.6 * math.exp(-0.3 * l)
        xa, ga, q, kb, vb, gb, k5, v5 = _inproj(hp, pre_g3, w_in_b, l, kv_p, depth)
        kv_p = (k5, v5)
        ya, cp, rp = _rglru(xa, ga, zero_buf, zero_h, conv_w, conv_b3, wr_bd, br3, wi_bd, bi3, lam3,
                            l, batch, seq)
        yb = _attn_prompt(q, kb, vb, gb, slab_p, lam_vec, subln_col, l, lam_init, batch, seq)
        hp = _outproj(ya, yb, hp, w_out_b, post_g3, l)
        cp_l.append(cp)
        rp_l.append(rp)
        xa, ga, q, kb, vb, gb, k5, v5 = _inproj(hs, pre_g3, w_in_b, l, kv_s, depth)
        kv_s = (k5, v5)
        ya, cs, rs = _rglru(xa, ga, state_conv8[l], state_rglru[l][:, None, :], conv_w, conv_b3,
                            wr_bd, br3, wi_bd, bi3, lam3, l, dec_batch, dec_seq)
        yb = _attn_sample(q, kb, vb, gb, cache_k, cache_v, slab_s, lam_vec, subln_row, l, lam_init,
                          dec_batch, dec_seq)
        hs = _outproj(ya, yb, hs, w_out_b, post_g3, l)
        cs_l.append(cs)
        rs_l.append(rs)

    y_prompt = hp.reshape(batch, seq, d)[:, N_META:]
    y_sample = hs.reshape(dec_batch, dec_seq, d)
    k_prompt = kv_p[0].reshape(depth, batch, seq, N_HEADS_B, 2 * DK)
    v_prompt = kv_p[1].reshape(depth, batch, seq, N_HEADS_B, DV)
    k_sample = kv_s[0].reshape(depth, dec_batch, dec_seq, N_HEADS_B, 2 * DK)
    v_sample = kv_s[1].reshape(depth, dec_batch, dec_seq, N_HEADS_B, DV)
    return (y_prompt, y_sample, k_prompt, v_prompt, jnp.stack(cp_l), jnp.stack(rp_l),
            k_sample, v_sample, jnp.stack(cs_l), jnp.stack(rs_l))
```

```python
import functools
import math

import jax
import jax.numpy as jnp
from jax import lax
from jax.experimental import pallas as pl
from jax.experimental.pallas import tpu as pltpu

F32 = jnp.float32
BF16 = jnp.bfloat16

CHUNK = 64
N_META = 16
N_BLOCKS_A = 16
CONV_W = 4
RG_C = 8.0
N_HEADS_B = 8
DK = 64
DV = 2 * DK
NUM_BUCKETS = 32
REL_MAX_DIST = 1024
EPS = 1e-6

LANES = 128
SUBLANES = 8
BF16_ROWS = 16
VMEM_LIMIT_BYTES = 56 * 1024 * 1024

LOG2E = math.log2(math.e)
Q_TILE = 256
GATE_BLOCK = 256


def _sigmoid(x):
    return 1.0 / (1.0 + jnp.exp(-x))


def _row_tile(m, cap=512):
    best = None
    for t in range(BF16_ROWS, min(m, cap) + 1, BF16_ROWS):
        if m % t == 0:
            best = t
    assert best is not None, m
    return best


def _bias_lookup(rel, tab_ref, h):
    half = NUM_BUCKETS // 2
    max_exact = half // 2
    ret = jnp.where(rel > 0, half, 0).astype(jnp.int32)
    n = jnp.abs(rel)
    nf = jnp.maximum(n, 1).astype(F32)
    large = max_exact + (jnp.log(nf / max_exact) / math.log(REL_MAX_DIST / max_exact)
                         * (half - max_exact)).astype(jnp.int32)
    large = jnp.minimum(large, half - 1)
    bucket = ret + jnp.where(n < max_exact, n, large)
    val = jnp.zeros(rel.shape, F32)
    for b in range(NUM_BUCKETS):
        val = jnp.where(bucket == b, tab_ref[b, h], val)
    return val


def _prompt_slab_kernel(tab_ref, out_ref, *, n_tiles):
    h = pl.program_id(0)
    seq = Q_TILE * n_tiles + N_META
    diag0 = Q_TILE * (n_tiles - 1) + N_META
    r0 = 0
    while r0 < seq:
        blk = min(LANES, seq - r0)
        r = lax.broadcasted_iota(jnp.int32, (blk, Q_TILE), 0) + r0
        c = lax.broadcasted_iota(jnp.int32, (blk, Q_TILE), 1)
        val = _bias_lookup(r - c - diag0, tab_ref, h) * LOG2E
        rp = r - diag0
        masked = jnp.logical_and(rp >= 0, (rp // CHUNK) > (c // CHUNK))
        out_ref[r0:r0 + blk, :] = jnp.where(masked, -jnp.inf, val)
        r0 += blk
    k = lax.broadcasted_iota(jnp.int32, (N_META, Q_TILE), 0)
    c = lax.broadcasted_iota(jnp.int32, (N_META, Q_TILE), 1)
    out_ref[seq:seq + N_META, :] = _bias_lookup(k - c, tab_ref, h) * LOG2E


def _sample_slab_kernel(tab_ref, out_ref, *, past, dec_seq):
    h = pl.program_id(0)
    width = out_ref.shape[-1]
    for c0 in range(0, width, 512):
        w = min(512, width - c0)
        t = lax.broadcasted_iota(jnp.int32, (dec_seq, w), 0)
        k = lax.broadcasted_iota(jnp.int32, (dec_seq, w), 1) + c0
        val = _bias_lookup(k - past - t, tab_ref, h) * LOG2E
        out_ref[:, c0:c0 + w] = jnp.where(k >= past + dec_seq, -jnp.inf, val)


def _bias_slabs(rel_bias, n_tiles, past, dec_seq):
    rows = Q_TILE * n_tiles + 2 * N_META
    smem = pl.BlockSpec(memory_space=pltpu.SMEM)
    prompt = pl.pallas_call(
        functools.partial(_prompt_slab_kernel, n_tiles=n_tiles),
        grid=(N_HEADS_B,),
        in_specs=[smem],
        out_specs=pl.BlockSpec((None, rows, Q_TILE), lambda h: (h, 0, 0)),
        out_shape=jax.ShapeDtypeStruct((N_HEADS_B, rows, Q_TILE), F32),
        name="prompt_bias_slab",
    )(rel_bias)
    width = past + LANES
    sample = pl.pallas_call(
        functools.partial(_sample_slab_kernel, past=past, dec_seq=dec_seq),
        grid=(N_HEADS_B,),
        in_specs=[smem],
        out_specs=pl.BlockSpec((None, dec_seq, width), lambda h: (h, 0, 0)),
        out_shape=jax.ShapeDtypeStruct((N_HEADS_B, dec_seq, width), F32),
        name="sample_bias_slab",
    )(rel_bias)
    return prompt, sample


def _inproj_kernel(x_ref, g_ref, w_ref, *rest, col_chunk):
    xa_ref, ga_ref, q_ref, kb_ref, vb_ref, gb_ref, k5_ref, v5_ref, u_s = rest[-9:]
    tm = x_ref.shape[0]
    x = x_ref[...]
    ms = jnp.mean(x * x, axis=-1, keepdims=True)
    u_s[...] = ((x * lax.rsqrt(ms + EPS)) * g_ref[...]).astype(BF16)
    col = 0
    for ref, scale, ref5 in ((xa_ref, None, None), (ga_ref, None, None),
                             (q_ref, (DK ** -0.5) * LOG2E, None),
                             (kb_ref, None, k5_ref), (vb_ref, None, v5_ref), (gb_ref, None, None)):
        width = ref.shape[-1]
        for c in range(0, width, col_chunk):
            acc = jnp.dot(u_s[...], w_ref[:, col + c:col + c + col_chunk], preferred_element_type=F32)
            if scale is not None:
                acc = acc * scale
            ref[:, c:c + col_chunk] = acc.astype(ref.dtype)
            if ref5 is not None:
                for j in range(col_chunk // DV):
                    head = c // DV + j
                    ref5[pl.ds(head, tm, stride=N_HEADS_B), :] = acc[:, j * DV:(j + 1) * DV]
        col += width


def _inproj(x2d, pre_g, w_in_b, layer, kv_stacks, depth):
    m, d = x2d.shape
    d_a = d // 2
    width = N_HEADS_B * DV
    in_cols = w_in_b.shape[-1]
    tm = _row_tile(m, 384)
    row = lambda w: pl.BlockSpec((tm, w), lambda i: (i, 0))
    stack = pl.BlockSpec((None, tm * N_HEADS_B, DV), lambda i: (layer, i, 0))
    in_specs = [
        row(d),
        pl.BlockSpec((None, 1, d), lambda i: (layer, 0, 0)),
        pl.BlockSpec((None, d, in_cols), lambda i: (layer, 0, 0), pipeline_mode=pl.Buffered(1)),
    ]
    args = [x2d, pre_g, w_in_b]
    aliases = {}
    if kv_stacks is not None:
        in_specs += [pl.BlockSpec(memory_space=pl.ANY)] * 2
        args += list(kv_stacks)
        aliases = {3: 6, 4: 7}
    out_shape = [
        jax.ShapeDtypeStruct((m, d_a), F32),
        jax.ShapeDtypeStruct((m, d_a), F32),
        jax.ShapeDtypeStruct((m, width), BF16),
        jax.ShapeDtypeStruct((m, width), BF16),
        jax.ShapeDtypeStruct((m, width), BF16),
        jax.ShapeDtypeStruct((m, width), BF16),
        jax.ShapeDtypeStruct((depth, m * N_HEADS_B, DV), F32),
        jax.ShapeDtypeStruct((depth, m * N_HEADS_B, DV), F32),
    ]
    out_specs = [row(d_a), row(d_a), row(width), row(width), row(width), row(width), stack, stack]
    return pl.pallas_call(
        functools.partial(_inproj_kernel, col_chunk=512),
        grid=(m // tm,),
        in_specs=in_specs,
        out_specs=out_specs,
        out_shape=out_shape,
        scratch_shapes=[pltpu.VMEM((tm, d), BF16)],
        input_output_aliases=aliases,
        compiler_params=pltpu.CompilerParams(
            dimension_semantics=("arbitrary",), vmem_limit_bytes=VMEM_LIMIT_BYTES),
        name="in_projection",
    )(*args)


def _rglru_kernel(xa_ref, ga_ref, buf_ref, h0_ref, cw_ref, cb_ref, wr_ref, br_ref, wi_ref, bi_ref,
                  lam_ref, y_ref, nbuf_ref, hl_ref, xp_s, a_s, b_s, hs_s, as_s, hn_s, *, seq, seg, g_block):
    c = xa_ref.shape[-1]
    nl = c // LANES
    lanes = lambda j: slice(j * LANES, (j + 1) * LANES)
    pad = SUBLANES * seg - seq
    for j in range(nl):
        xp_s[j, 0:SUBLANES, :] = buf_ref[:, lanes(j)]
        xp_s[j, SUBLANES:SUBLANES + seq, :] = xa_ref[:, lanes(j)]
        if pad:
            xp_s[j, SUBLANES + seq:, :] = jnp.zeros((pad, LANES), F32)
        nbuf_ref[:, lanes(j)] = xp_s[j, seq:seq + SUBLANES, :]

    neg = -lam_ref[...]
    softplus = jnp.maximum(neg, 0.0) + jnp.log1p(jnp.exp(-jnp.abs(neg)))
    cneg = -RG_C * softplus
    cb = cb_ref[...]
    taps = [cw_ref[j:j + 1, :] for j in range(CONV_W)]

    cache = {}

    def rows_at(g):
        if g not in cache:
            cache[g] = jnp.concatenate(
                [xp_s[j, pl.ds(SUBLANES + g, SUBLANES, stride=seg), :] for j in range(nl)], axis=1)
        return cache[g]

    for g0 in range(0, seg, g_block):
        pieces = []
        for g in range(g0, g0 + g_block):
            xc = cb
            for j in range(CONV_W):
                xc = xc + rows_at(g - (CONV_W - 1) + j) * taps[j]
            pieces.append(xc)
        xc = jnp.concatenate(pieces, axis=0)
        xcb = xc.astype(BF16)
        r = _sigmoid(jnp.dot(xcb, wr_ref[...], preferred_element_type=F32) + br_ref[...])
        ig = _sigmoid(jnp.dot(xcb, wi_ref[...], preferred_element_type=F32) + bi_ref[...])
        log_a = r * cneg
        a = jnp.exp(log_a)
        lo, hi = g0 * SUBLANES, (g0 + g_block) * SUBLANES
        a_s[lo:hi, :] = a
        e = -jnp.tanh(log_a) * (a * a + 1.0)
        root = jnp.where(e > 0.0, e * lax.rsqrt(e), 0.0)
        b_s[lo:hi, :] = root * (ig * xc)

    rowid = lax.broadcasted_iota(jnp.int32, (SUBLANES, c), 0)
    first = rowid == 0

    def scan_body(g, carry):
        h, acc = carry
        r = pl.multiple_of(g * SUBLANES, SUBLANES)
        a = a_s[pl.ds(r, SUBLANES), :]
        h = a * h + b_s[pl.ds(r, SUBLANES), :]
        acc = a * acc
        hs_s[pl.ds(r, SUBLANES), :] = h
        as_s[pl.ds(r, SUBLANES), :] = acc
        return h, acc

    h_init = jnp.where(first, jnp.broadcast_to(h0_ref[...], (SUBLANES, c)), 0.0)
    end_h, end_a = lax.fori_loop(0, seg, scan_body, (h_init, jnp.ones((SUBLANES, c), F32)), unroll=8)

    d = jnp.zeros((SUBLANES, c), F32)
    for _ in range(SUBLANES - 1):
        d = jnp.where(first, 0.0, pltpu.roll(end_h + end_a * d, 1, 0))

    def fix_body(g, carry):
        r = pl.multiple_of(g * SUBLANES, SUBLANES)
        h = hs_s[pl.ds(r, SUBLANES), :] + as_s[pl.ds(r, SUBLANES), :] * d
        for j in range(nl):
            hn_s[j, pl.ds(g, SUBLANES, stride=seg), :] = h[:, lanes(j)]
        return carry

    lax.fori_loop(0, seg, fix_body, 0, unroll=8)

    for j in range(nl):
        ga = ga_ref[:, lanes(j)]
        y_ref[:, lanes(j)] = (hn_s[j, 0:seq, :] * (ga * _sigmoid(ga))).astype(y_ref.dtype)
        hl_ref[:, lanes(j)] = hn_s[j, seq - 1:seq, :]


def _rglru(xa, ga, buf8, h0, conv_w, conv_b, wr_bd, br, wi_bd, bi, lam, layer, batch, seq):
    m, d_a = xa.shape
    xa3 = xa.reshape(batch, seq, d_a)
    ga3 = ga.reshape(batch, seq, d_a)
    nc = d_a // GATE_BLOCK
    nl = GATE_BLOCK // LANES
    seg = pl.cdiv(pl.cdiv(seq, SUBLANES), SUBLANES) * SUBLANES
    g_block = max(t for t in range(1, min(seg, 44) + 1) if seg % t == 0)
    chan = lambda rows: pl.BlockSpec((None, rows, GATE_BLOCK), lambda b, c: (b, 0, c))
    par = lambda rows: pl.BlockSpec((None, rows, GATE_BLOCK), lambda b, c: (layer, 0, c))
    wspec = pl.BlockSpec((None, None, GATE_BLOCK, GATE_BLOCK), lambda b, c: (layer, c, 0, 0))
    y, nbuf, hl = pl.pallas_call(
        functools.partial(_rglru_kernel, seq=seq, seg=seg, g_block=g_block),
        grid=(batch, nc),
        in_specs=[chan(seq), chan(seq), chan(SUBLANES), chan(1), par(CONV_W), par(1),
                  wspec, par(1), wspec, par(1), par(1)],
        out_specs=[chan(seq), chan(SUBLANES), chan(1)],
        out_shape=[jax.ShapeDtypeStruct((batch, seq, d_a), BF16),
                   jax.ShapeDtypeStruct((batch, SUBLANES, d_a), F32),
                   jax.ShapeDtypeStruct((batch, 1, d_a), F32)],
        scratch_shapes=[pltpu.VMEM((nl, SUBLANES * (seg + 1), LANES), F32),
                        pltpu.VMEM((SUBLANES * seg, GATE_BLOCK), F32),
                        pltpu.VMEM((SUBLANES * seg, GATE_BLOCK), F32),
                        pltpu.VMEM((SUBLANES * seg, GATE_BLOCK), F32),
                        pltpu.VMEM((SUBLANES * seg, GATE_BLOCK), F32),
                        pltpu.VMEM((nl, SUBLANES * seg, LANES), F32)],
        compiler_params=pltpu.CompilerParams(
            dimension_semantics=("arbitrary", "arbitrary"), vmem_limit_bytes=VMEM_LIMIT_BYTES),
        name="rglru_mixer",
    )(xa3, ga3, buf8, h0, conv_w, conv_b, wr_bd, br, wi_bd, bi, lam)
    return y.reshape(m, d_a), nbuf[:, SUBLANES - (CONV_W - 1):, :], hl[:, 0, :]


def _diff_lambda(lv, lam_init):
    s1 = jnp.sum(lv[0:1, :] * lv[1:2, :], axis=-1, keepdims=True)
    s2 = jnp.sum(lv[2:3, :] * lv[3:4, :], axis=-1, keepdims=True)
    return jnp.exp(s1) - jnp.exp(s2) + lam_init


def _attn_prompt_kernel(lv_ref, q_ref, k_ref, v_ref, gb_ref, bt_ref, g_ref, o_ref,
                        q1t_s, q2t_s, vt_s, acc_s, s_s, *, seq, n_tiles, lam_init):
    gap = LANES - N_META
    lam = _diff_lambda(lv_ref[...], lam_init)

    row = lax.broadcasted_iota(jnp.int32, (2 * DK, LANES), 0)
    zero = jnp.zeros((2 * DK, LANES), BF16)
    pad = jnp.zeros((gap, DV), BF16)
    for j in range(2 * n_tiles + 1):
        cols = slice(LANES * j, LANES * (j + 1))
        if j == 0:
            qb = jnp.concatenate([q_ref[0:N_META, :], pad], axis=0)
            vb = jnp.concatenate([v_ref[0:N_META, :], pad], axis=0)
        else:
            rows = slice(N_META + LANES * (j - 1), N_META + LANES * j)
            qb = q_ref[rows, :]
            vb = v_ref[rows, :]
        qt = qb.T
        q1t_s[:, cols] = jnp.where(row < DK, qt, zero)
        q2t_s[:, cols] = jnp.where(row >= DK, qt, zero)
        vt_s[:, cols] = vb.T

    gcol = g_ref[...] * (1.0 - lam_init)
    meta_pad = jnp.zeros((gap, Q_TILE), BF16)

    tiles = [(0, LANES, 0, seq, 0, N_META)]
    for i in range(n_tiles):
        tiles.append((LANES + Q_TILE * i, Q_TILE, i + 1, Q_TILE * (n_tiles - 1 - i),
                      N_META + Q_TILE * i, Q_TILE))
    q_maps = (q1t_s, q2t_s)
    state = {}

    def key_rows(t, j):
        if j == 0:
            return 0, N_META + (Q_TILE if t else 0)
        return N_META + Q_TILE * j, Q_TILE

    def scores(t, j, slot):
        q_col, n_q, _, b_lo, _, _ = tiles[t]
        r0, nr = key_rows(t, j)
        cms = []
        for m in range(2):
            s = jnp.dot(k_ref[r0:r0 + nr, :], q_maps[m][:, q_col:q_col + n_q],
                        preferred_element_type=F32)
            s = s + bt_ref[b_lo + r0:b_lo + r0 + nr, 0:n_q]
            s_s[slot, m, 0:nr, 0:n_q] = s
            cms.append(jnp.max(s, axis=0, keepdims=True))
        return cms

    def fold(t, j, slot, cms):
        _, n_q, _, _, _, _ = tiles[t]
        r0, nr = key_rows(t, j)
        ops = []
        for m in range(2):
            if j == 0:
                m_new, alpha = cms[m], None
                s = s_s[slot, m, 0:nr, 0:n_q]
                p = jnp.exp2(s - m_new)
                den = jnp.sum(p, axis=0, keepdims=True)
                pb = p.astype(BF16)
                parts = [pb[0:N_META], meta_pad[:, 0:n_q]]
                if nr > N_META:
                    parts.append(pb[N_META:nr])
                pb = jnp.concatenate(parts, axis=0)
                c0, nc = 0, LANES + nr - N_META
            else:
                mx, den = state[t, m]
                m_new = jnp.maximum(mx, cms[m])
                alpha = jnp.exp2(mx - m_new)
                p = jnp.exp2(s_s[slot, m, 0:nr, 0:n_q] - m_new)
                den = alpha * den + jnp.sum(p, axis=0, keepdims=True)
                pb = p.astype(BF16)
                c0, nc = LANES + Q_TILE * j, Q_TILE
            state[t, m] = (m_new, den)
            ops.append((alpha, pb, c0, nc))
        for m, (alpha, pb, c0, nc) in enumerate(ops):
            o = jnp.dot(vt_s[:, c0:c0 + nc], pb, preferred_element_type=F32)
            if alpha is None:
                acc_s[t, m, :, 0:n_q] = o
            else:
                acc_s[t, m, :, 0:n_q] = alpha * acc_s[t, m, :, 0:n_q] + o

    def finish(t):
        _, n_q, _, _, q_lo, out_rows = tiles[t]
        outs = [acc_s[t, m, :, 0:n_q] * (1.0 / state[t, m][1]) for m in range(2)]
        ot = outs[0] - lam * outs[1]
        ms = jnp.mean(ot * ot, axis=0, keepdims=True)
        on = (ot * lax.rsqrt(ms + EPS)) * gcol
        o_rows = on.T[0:out_rows, :]
        gb = gb_ref[q_lo:q_lo + out_rows, :].astype(F32)
        o_ref[q_lo:q_lo + out_rows, :] = (o_rows * (gb * _sigmoid(gb))).astype(o_ref.dtype)

    jobs = [(0, 0)] + [(t, j) for _, t, j in sorted(
        ((j + 0.5) / tiles[t][2], t, j) for t in range(1, len(tiles)) for j in range(tiles[t][2]))]
    cms_next = scores(*jobs[0], 0)
    for n, (t, j) in enumerate(jobs):
        cms = cms_next
        if n + 1 < len(jobs):
            cms_next = scores(*jobs[n + 1], (n + 1) % 2)
        fold(t, j, n % 2, cms)
        if j == max(tiles[t][2] - 1, 0):
            finish(t)


def _attn_prompt(q, kb, vb, gb, slab, lam_vec, subln_col, layer, lam_init, batch, seq):
    m, width = q.shape
    n_tiles = (seq - N_META) // Q_TILE
    assert N_META + n_tiles * Q_TILE == seq
    tp = (2 * n_tiles + 1) * LANES
    to3 = lambda a: a.reshape(batch, seq, width)
    head = pl.BlockSpec((None, seq, DV), lambda h, b: (b, 0, h))
    out = pl.pallas_call(
        functools.partial(_attn_prompt_kernel, seq=seq, n_tiles=n_tiles, lam_init=lam_init),
        grid=(N_HEADS_B, batch),
        in_specs=[pl.BlockSpec((None, 4, DK), lambda h, b: (layer, 0, 0)),
                  head, head, head, head,
                  pl.BlockSpec((None, seq + N_META, Q_TILE), lambda h, b: (h, 0, 0)),
                  pl.BlockSpec((None, DV, 1), lambda h, b: (layer, 0, 0))],
        out_specs=head,
        out_shape=jax.ShapeDtypeStruct((batch, seq, width), BF16),
        scratch_shapes=[pltpu.VMEM((2 * DK, tp), BF16),
                        pltpu.VMEM((2 * DK, tp), BF16),
                        pltpu.VMEM((DV, tp), BF16),
                        pltpu.VMEM((n_tiles + 1, 2, DV, Q_TILE), F32),
                        pltpu.VMEM((2, 2, N_META + Q_TILE, Q_TILE), F32)],
        compiler_params=pltpu.CompilerParams(
            dimension_semantics=("arbitrary", "arbitrary"), vmem_limit_bytes=VMEM_LIMIT_BYTES),
        name="diff_attention_prompt",
    )(lam_vec, to3(q), to3(kb), to3(vb), to3(gb), slab, subln_col)
    return out.reshape(m, width)


def _attn_sample_kernel(lv_ref, q_ref, kc_ref, vc_ref, kn_ref, vn_ref, gb_ref, bs_ref, g_ref, o_ref,
                        kb_s, vb_s, s_s, *, past, dec_seq, lam_init):
    lam = _diff_lambda(lv_ref[...], lam_init)
    width = past + LANES
    lanes = lambda h: slice(h * DV, (h + 1) * DV)
    lane = lax.broadcasted_iota(jnp.int32, (dec_seq, DV), 1)
    zero = jnp.zeros((dec_seq, DV), BF16)
    tail = jnp.zeros((LANES - dec_seq, DV), BF16)

    def scores(h, slot):
        for dst, cache, new in ((kb_s, kc_ref, kn_ref), (vb_s, vc_ref, vn_ref)):
            dst[slot, 0:past, :] = cache[pl.ds(h, past, stride=N_HEADS_B), :].astype(BF16)
            dst[slot, past:past + dec_seq, :] = new[:, lanes(h)]
            dst[slot, past + dec_seq:width, :] = tail
        qv = q_ref[:, lanes(h)]
        for m, qm in enumerate((jnp.where(lane < DK, qv, zero), jnp.where(lane >= DK, qv, zero))):
            s = lax.dot_general(qm, kb_s[slot], (((1,), (1,)), ((), ())), preferred_element_type=F32)
            s_s[slot, m] = s + bs_ref[h]

    def fold(h, slot):
        probs = []
        for m in range(2):
            s = s_s[slot, m]
            mx = jnp.max(s, axis=-1, keepdims=True)
            p = jnp.exp2(s - mx)
            probs.append(p * (1.0 / jnp.sum(p, axis=-1, keepdims=True)))
        pd = (probs[0] - lam * probs[1]).astype(BF16)
        o = jnp.dot(pd, vb_s[slot], preferred_element_type=F32)
        ms = jnp.mean(o * o, axis=-1, keepdims=True)
        on = ((o * lax.rsqrt(ms + EPS)) * g_ref[...]) * (1.0 - lam_init)
        gb = gb_ref[:, lanes(h)].astype(F32)
        o_ref[:, lanes(h)] = (on * (gb * _sigmoid(gb))).astype(o_ref.dtype)

    scores(0, 0)
    for h in range(N_HEADS_B):
        if h + 1 < N_HEADS_B:
            scores(h + 1, (h + 1) % 2)
        fold(h, h % 2)


def _attn_sample(q, kb, vb, gb, cache_k, cache_v, slab, lam_vec, subln_row, layer, lam_init,
                 batch, dec_seq):
    m, width = q.shape
    depth, _, past = cache_k.shape[:3]
    ck = cache_k.reshape(depth, batch, past * N_HEADS_B, DV)
    cv = cache_v.reshape(depth, batch, past * N_HEADS_B, DV)
    rows = pl.BlockSpec((dec_seq, width), lambda b: (b, 0))
    cache = pl.BlockSpec((None, None, past * N_HEADS_B, DV), lambda b: (layer, b, 0, 0))
    return pl.pallas_call(
        functools.partial(_attn_sample_kernel, past=past, dec_seq=dec_seq, lam_init=lam_init),
        grid=(batch,),
        in_specs=[pl.BlockSpec((None, 4, DK), lambda b: (layer, 0, 0)),
                  rows, cache, cache, rows, rows, rows,
                  pl.BlockSpec((N_HEADS_B, dec_seq, past + LANES), lambda b: (0, 0, 0)),
                  pl.BlockSpec((None, 1, DV), lambda b: (layer, 0, 0))],
        out_specs=rows,
        out_shape=jax.ShapeDtypeStruct((m, width), BF16),
        scratch_shapes=[pltpu.VMEM((2, past + LANES, DV), BF16),
                        pltpu.VMEM((2, past + LANES, DV), BF16),
                        pltpu.VMEM((2, 2, dec_seq, past + LANES), F32)],
        compiler_params=pltpu.CompilerParams(
            dimension_semantics=("arbitrary",), vmem_limit_bytes=VMEM_LIMIT_BYTES),
        name="diff_attention_sample",
    )(lam_vec, q, ck, cv, kb, vb, gb, slab, subln_row)


def _outproj_kernel(ya_ref, yb_ref, x_ref, w_ref, g_ref, o_ref, y_s, *, col_chunk):
    d_a = ya_ref.shape[-1]
    d = o_ref.shape[-1]
    for c in range(0, d, col_chunk):
        y_s[:, c:c + col_chunk] = (
            jnp.dot(ya_ref[...], w_ref[0:d_a, c:c + col_chunk], preferred_element_type=F32)
            + jnp.dot(yb_ref[...], w_ref[d_a:, c:c + col_chunk], preferred_element_type=F32))
    y = y_s[...]
    ms = jnp.mean(y * y, axis=-1, keepdims=True)
    o_ref[...] = x_ref[...] + (y * lax.rsqrt(ms + EPS)) * g_ref[...]


def _outproj(ya, yb, x2d, w_out_b, post_g, layer):
    m, d = x2d.shape
    tm = _row_tile(m, 384)
    half = pl.BlockSpec((tm, d // 2), lambda i: (i, 0))
    full = pl.BlockSpec((tm, d), lambda i: (i, 0))
    return pl.pallas_call(
        functools.partial(_outproj_kernel, col_chunk=512),
        grid=(m // tm,),
        in_specs=[half, half, full,
                  pl.BlockSpec((None, d, d), lambda i: (layer, 0, 0)),
                  pl.BlockSpec((None, 1, d), lambda i: (layer, 0, 0))],
        out_specs=full,
        out_shape=jax.ShapeDtypeStruct((m, d), F32),
        scratch_shapes=[pltpu.VMEM((tm, d), F32)],
        compiler_params=pltpu.CompilerParams(
            dimension_semantics=("arbitrary",), vmem_limit_bytes=VMEM_LIMIT_BYTES),
        name="out_projection",
    )(ya, yb, x2d, w_out_b, post_g)


def _block_diag_gates(w):
    depth, nb, bs, _ = w.shape
    per = GATE_BLOCK // bs
    w5 = w.reshape(depth, nb // per, per, bs, bs)
    eye = jnp.eye(per, dtype=w.dtype)
    bd = jnp.einsum('lgaij,ab->lgaibj', w5, eye)
    return bd.reshape(depth, nb // per, GATE_BLOCK, GATE_BLOCK).astype(BF16)


def kernel(x_prompt, x_sample, cache_k, cache_v, state_conv, state_rglru, meta, rel_bias, pre_g, post_g,
           w_in, conv_w, conv_b, gate_r_w, gate_r_b, gate_i_w, gate_i_b, rglru_lam, lam_q1, lam_k1,
           lam_q2, lam_k2, subln_g, w_out):
    batch, seq0, d = x_prompt.shape
    dec_batch, dec_seq, _ = x_sample.shape
    depth = w_in.shape[0]
    past = cache_k.shape[2]
    d_a = d // 2
    seq = seq0 + N_META
    n_tiles = seq0 // Q_TILE

    hp = jnp.concatenate(
        [jnp.broadcast_to(meta.astype(x_prompt.dtype)[None], (batch, N_META, d)), x_prompt],
        axis=1).reshape(batch * seq, d)
    hs = x_sample.reshape(dec_batch * dec_seq, d)

    w_in_b = w_in.astype(BF16)
    w_out_b = w_out.astype(BF16)
    wr_bd = _block_diag_gates(gate_r_w)
    wi_bd = _block_diag_gates(gate_i_w)
    pre_g3 = pre_g.reshape(depth, 1, d)
    post_g3 = post_g.reshape(depth, 1, d)
    conv_b3 = conv_b.reshape(depth, 1, d_a)
    br3 = gate_r_b.reshape(depth, 1, d_a)
    bi3 = gate_i_b.reshape(depth, 1, d_a)
    lam3 = rglru_lam.reshape(depth, 1, d_a)
    lam_vec = jnp.stack([lam_q1, lam_k1, lam_q2, lam_k2], axis=1)
    subln_col = subln_g.reshape(depth, DV, 1)
    subln_row = subln_g.reshape(depth, 1, DV)

    slab_p, slab_s = _bias_slabs(rel_bias, n_tiles, past, dec_seq)

    zero_buf = jnp.zeros((batch, SUBLANES, d_a), F32)
    zero_h = jnp.zeros((batch, 1, d_a), F32)
    pad_rows = SUBLANES - (CONV_W - 1)
    state_conv8 = jnp.pad(state_conv, ((0, 0), (0, 0), (pad_rows, 0), (0, 0)))

    kv_p = None
    kv_s = None
    cp_l, rp_l, cs_l, rs_l = [], [], [], []
    for l in range(depth):
        lam_init = 0.8 - 0.6 * math.exp(-0.3 * l)
        xa, ga, q, kb, vb, gb, k5, v5 = _inproj(hp, pre_g3, w_in_b, l, kv_p, depth)
        kv_p = (k5, v5)
        ya, cp, rp = _rglru(xa, ga, zero_buf, zero_h, conv_w, conv_b3, wr_bd, br3, wi_bd, bi3, lam3,
                            l, batch, seq)
        yb = _attn_prompt(q, kb, vb, gb, slab_p, lam_vec, subln_col, l, lam_init, batch, seq)
        hp = _outproj(ya, yb, hp, w_out_b, post_g3, l)
        cp_l.append(cp)
        rp_l.append(rp)
        xa, ga, q, kb, vb, gb, k5, v5 = _inproj(hs, pre_g3, w_in_b, l, kv_s, depth)
        kv_s = (k5, v5)
        ya, cs, rs = _rglru(xa, ga, state_conv8[l], state_rglru[l][:, None, :], conv_w, conv_b3,
                            wr_bd, br3, wi_bd, bi3, lam3, l, dec_batch, dec_seq)
        yb = _attn_sample(q, kb, vb, gb, cache_k, cache_v, slab_s, lam_vec, subln_row, l, lam_init,
                          dec_batch, dec_seq)
        hs = _outproj(ya, yb, hs, w_out_b, post_g3, l)
        cs_l.append(cs)
        rs_l.append(rs)

    y_prompt = hp.reshape(batch, seq, d)[:, N_META:]
    y_sample = hs.reshape(dec_batch, dec_seq, d)
    k_prompt = kv_p[0].reshape(depth, batch, seq, N_HEADS_B, 2 * DK)
    v_prompt = kv_p[1].reshape(depth, batch, seq, N_HEADS_B, DV)
    k_sample = kv_s[0].reshape(depth, dec_batch, dec_seq, N_HEADS_B, 2 * DK)
    v_sample = kv_s[1].reshape(depth, dec_batch, dec_seq, N_HEADS_B, DV)
    return (y_prompt, y_sample, k_prompt, v_prompt, jnp.stack(cp_l), jnp.stack(rp_l),
            k_sample, v_sample, jnp.stack(cs_l), jnp.stack(rs_l))
```

```python
import functools
import math

import jax
import jax.numpy as jnp
from jax import lax
from jax.experimental import pallas as pl
from jax.experimental.pallas import tpu as pltpu

F32 = jnp.float32
BF16 = jnp.bfloat16

CHUNK = 64
N_META = 16
N_BLOCKS_A = 16
CONV_W = 4
RG_C = 8.0
N_HEADS_B = 8
DK = 64
DV = 2 * DK
NUM_BUCKETS = 32
REL_MAX_DIST = 1024
EPS = 1e-6

LANES = 128
SUBLANES = 8
BF16_ROWS = 16
VMEM_LIMIT_BYTES = 56 * 1024 * 1024

LOG2E = math.log2(math.e)
Q_TILE = 256
GATE_BLOCK = 256
RGLRU_SCRATCH_BYTES = 16 * 1024 * 1024


def _sigmoid(x):
    return 1.0 / (1.0 + jnp.exp(-x))


def _row_tile(m, cap=512):
    best = None
    for t in range(BF16_ROWS, min(m, cap) + 1, BF16_ROWS):
        if m % t == 0:
            best = t
    assert best is not None, m
    return best


def _bias_lookup(rel, tab_ref, h):
    half = NUM_BUCKETS // 2
    max_exact = half // 2
    ret = jnp.where(rel > 0, half, 0).astype(jnp.int32)
    n = jnp.abs(rel)
    nf = jnp.maximum(n, 1).astype(F32)
    large = max_exact + (jnp.log(nf / max_exact) / math.log(REL_MAX_DIST / max_exact)
                         * (half - max_exact)).astype(jnp.int32)
    large = jnp.minimum(large, half - 1)
    bucket = ret + jnp.where(n < max_exact, n, large)
    val = jnp.zeros(rel.shape, F32)
    for b in range(NUM_BUCKETS):
        val = jnp.where(bucket == b, tab_ref[b, h], val)
    return val


def _prompt_slab_kernel(tab_ref, out_ref, *, n_tiles):
    h = pl.program_id(0)
    seq = Q_TILE * n_tiles + N_META
    diag0 = Q_TILE * (n_tiles - 1) + N_META
    r0 = 0
    while r0 < seq:
        blk = min(LANES, seq - r0)
        r = lax.broadcasted_iota(jnp.int32, (blk, Q_TILE), 0) + r0
        c = lax.broadcasted_iota(jnp.int32, (blk, Q_TILE), 1)
        val = _bias_lookup(r - c - diag0, tab_ref, h) * LOG2E
        rp = r - diag0
        masked = jnp.logical_and(rp >= 0, (rp // CHUNK) > (c // CHUNK))
        out_ref[r0:r0 + blk, :] = jnp.where(masked, -jnp.inf, val)
        r0 += blk
    k = lax.broadcasted_iota(jnp.int32, (N_META, Q_TILE), 0)
    c = lax.broadcasted_iota(jnp.int32, (N_META, Q_TILE), 1)
    out_ref[seq:seq + N_META, :] = _bias_lookup(k - c, tab_ref, h) * LOG2E


def _sample_slab_kernel(tab_ref, out_ref, *, past, dec_seq):
    h = pl.program_id(0)
    width = out_ref.shape[-1]
    for c0 in range(0, width, 512):
        w = min(512, width - c0)
        t = lax.broadcasted_iota(jnp.int32, (dec_seq, w), 0)
        k = lax.broadcasted_iota(jnp.int32, (dec_seq, w), 1) + c0
        val = _bias_lookup(k - past - t, tab_ref, h) * LOG2E
        out_ref[:, c0:c0 + w] = jnp.where(k >= past + dec_seq, -jnp.inf, val)


def _bias_slabs(rel_bias, n_tiles, past, dec_seq):
    rows = Q_TILE * n_tiles + 2 * N_META
    smem = pl.BlockSpec(memory_space=pltpu.SMEM)
    prompt = pl.pallas_call(
        functools.partial(_prompt_slab_kernel, n_tiles=n_tiles),
        grid=(N_HEADS_B,),
        in_specs=[smem],
        out_specs=pl.BlockSpec((None, rows, Q_TILE), lambda h: (h, 0, 0)),
        out_shape=jax.ShapeDtypeStruct((N_HEADS_B, rows, Q_TILE), F32),
        name="prompt_bias_slab",
    )(rel_bias)
    width = past + LANES
    sample = pl.pallas_call(
        functools.partial(_sample_slab_kernel, past=past, dec_seq=dec_seq),
        grid=(N_HEADS_B,),
        in_specs=[smem],
        out_specs=pl.BlockSpec((None, dec_seq, width), lambda h: (h, 0, 0)),
        out_shape=jax.ShapeDtypeStruct((N_HEADS_B, dec_seq, width), F32),
        name="sample_bias_slab",
    )(rel_bias)
    return prompt, sample


def _inproj_kernel(x_ref, g_ref, w_ref, *rest, col_chunk):
    xa_ref, ga_ref, q_ref, kb_ref, vb_ref, gb_ref, k5_ref, v5_ref, u_s = rest[-9:]
    tm = x_ref.shape[0]

    def normalise(lo, n):
        x = x_ref[lo:lo + n, :]
        ms = jnp.mean(x * x, axis=-1, keepdims=True)
        u_s[lo:lo + n, :] = ((x * lax.rsqrt(ms + EPS)) * g_ref[...]).astype(BF16)

    chunks = []
    col = 0
    for ref, scale, ref5 in ((xa_ref, None, None), (ga_ref, None, None),
                             (q_ref, (DK ** -0.5) * LOG2E, None),
                             (kb_ref, None, k5_ref), (vb_ref, None, v5_ref), (gb_ref, None, None)):
        for c in range(0, ref.shape[-1], col_chunk):
            chunks.append((ref, scale, ref5, col + c, c))
        col += ref.shape[-1]

    def project(lo, n, chunk):
        ref, scale, ref5, wcol, c = chunk
        acc = jnp.dot(u_s[lo:lo + n, :], w_ref[:, wcol:wcol + col_chunk], preferred_element_type=F32)
        if scale is not None:
            acc = acc * scale
        ref[lo:lo + n, c:c + col_chunk] = acc.astype(ref.dtype)
        if ref5 is not None:
            for j in range(col_chunk // DV):
                head = c // DV + j
                ref5[pl.ds(lo * N_HEADS_B + head, n, stride=N_HEADS_B), :] = acc[:, j * DV:(j + 1) * DV]

    normalise(0, tm)
    for chunk in chunks:
        project(0, tm, chunk)


def _inproj(x2d, pre_g, w_in_b, layer, kv_stacks, depth):
    m, d = x2d.shape
    d_a = d // 2
    width = N_HEADS_B * DV
    in_cols = w_in_b.shape[-1]
    tm = _row_tile(m, 384)
    row = lambda w: pl.BlockSpec((tm, w), lambda i: (i, 0))
    stack = pl.BlockSpec((None, tm * N_HEADS_B, DV), lambda i: (layer, i, 0))
    in_specs = [
        row(d),
        pl.BlockSpec((None, 1, d), lambda i: (layer, 0, 0)),
        pl.BlockSpec((None, d, in_cols), lambda i: (layer, 0, 0), pipeline_mode=pl.Buffered(1)),
    ]
    args = [x2d, pre_g, w_in_b]
    aliases = {}
    if kv_stacks is not None:
        in_specs += [pl.BlockSpec(memory_space=pl.ANY)] * 2
        args += list(kv_stacks)
        aliases = {3: 6, 4: 7}
    out_shape = [
        jax.ShapeDtypeStruct((m, d_a), F32),
        jax.ShapeDtypeStruct((m, d_a), F32),
        jax.ShapeDtypeStruct((m, width), BF16),
        jax.ShapeDtypeStruct((m, width), BF16),
        jax.ShapeDtypeStruct((m, width), BF16),
        jax.ShapeDtypeStruct((m, width), BF16),
        jax.ShapeDtypeStruct((depth, m * N_HEADS_B, DV), F32),
        jax.ShapeDtypeStruct((depth, m * N_HEADS_B, DV), F32),
    ]
    out_specs = [row(d_a), row(d_a), row(width), row(width), row(width), row(width), stack, stack]
    return pl.pallas_call(
        functools.partial(_inproj_kernel, col_chunk=512),
        grid=(m // tm,),
        in_specs=in_specs,
        out_specs=out_specs,
        out_shape=out_shape,
        scratch_shapes=[pltpu.VMEM((tm, d), BF16)],
        input_output_aliases=aliases,
        compiler_params=pltpu.CompilerParams(
            dimension_semantics=("arbitrary",), vmem_limit_bytes=VMEM_LIMIT_BYTES),
        name="in_projection",
    )(*args)


def _rglru_kernel(xa_ref, ga_ref, buf_ref, h0_ref, cw_ref, cb_ref, wr_ref, br_ref, wi_ref, bi_ref,
                  lam_ref, y_ref, nbuf_ref, hl_ref, xp_s, a_s, b_s, hs_s, as_s, hn_s, *, seq, seg, g_block):
    c = xa_ref.shape[-1]
    nl = c // LANES
    lanes = lambda j: slice(j * LANES, (j + 1) * LANES)
    pad = SUBLANES * seg - seq
    for j in range(nl):
        xp_s[j, 0:SUBLANES, :] = buf_ref[:, lanes(j)]
        xp_s[j, SUBLANES:SUBLANES + seq, :] = xa_ref[:, lanes(j)]
        if pad:
            xp_s[j, SUBLANES + seq:, :] = jnp.zeros((pad, LANES), F32)
        nbuf_ref[:, lanes(j)] = xp_s[j, seq:seq + SUBLANES, :]

    neg = -lam_ref[...]
    softplus = jnp.maximum(neg, 0.0) + jnp.log1p(jnp.exp(-jnp.abs(neg)))
    cneg = -RG_C * softplus
    cb = cb_ref[...]
    taps = [cw_ref[j:j + 1, :] for j in range(CONV_W)]

    cache = {}

    def rows_at(g):
        if g not in cache:
            cache[g] = jnp.concatenate(
                [xp_s[j, pl.ds(SUBLANES + g, SUBLANES, stride=seg), :] for j in range(nl)], axis=1)
        return cache[g]

    for g0 in range(0, seg, g_block):
        pieces = []
        for g in range(g0, g0 + g_block):
            xc = cb
            for j in range(CONV_W):
                xc = xc + rows_at(g - (CONV_W - 1) + j) * taps[j]
            pieces.append(xc)
        xc = jnp.concatenate(pieces, axis=0)
        xcb = xc.astype(BF16)

        def gate(w_ref, b_ref):
            z = [jnp.dot(xcb[:, k * GATE_BLOCK:(k + 1) * GATE_BLOCK], w_ref[k],
                         preferred_element_type=F32) for k in range(c // GATE_BLOCK)]
            return _sigmoid(jnp.concatenate(z, axis=1) + b_ref[...])

        r = gate(wr_ref, br_ref)
        ig = gate(wi_ref, bi_ref)
        log_a = r * cneg
        a = jnp.exp(log_a)
        lo, hi = g0 * SUBLANES, (g0 + g_block) * SUBLANES
        a_s[lo:hi, :] = a
        e = -jnp.tanh(log_a) * (a * a + 1.0)
        root = jnp.where(e > 0.0, e * lax.rsqrt(e), 0.0)
        b_s[lo:hi, :] = root * (ig * xc)

    rowid = lax.broadcasted_iota(jnp.int32, (SUBLANES, c), 0)
    first = rowid == 0

    def scan_body(g, carry):
        h, acc = carry
        r = pl.multiple_of(g * SUBLANES, SUBLANES)
        a = a_s[pl.ds(r, SUBLANES), :]
        h = a * h + b_s[pl.ds(r, SUBLANES), :]
        acc = a * acc
        hs_s[pl.ds(r, SUBLANES), :] = h
        as_s[pl.ds(r, SUBLANES), :] = acc
        return h, acc

    h_init = jnp.where(first, jnp.broadcast_to(h0_ref[...], (SUBLANES, c)), 0.0)
    end_h, end_a = lax.fori_loop(0, seg, scan_body, (h_init, jnp.ones((SUBLANES, c), F32)), unroll=8)

    d = jnp.zeros((SUBLANES, c), F32)
    for _ in range(SUBLANES - 1):
        d = jnp.where(first, 0.0, pltpu.roll(end_h + end_a * d, 1, 0))

    def fix_body(g, carry):
        r = pl.multiple_of(g * SUBLANES, SUBLANES)
        h = hs_s[pl.ds(r, SUBLANES), :] + as_s[pl.ds(r, SUBLANES), :] * d
        for j in range(nl):
            hn_s[j, pl.ds(g, SUBLANES, stride=seg), :] = h[:, lanes(j)]
        return carry

    lax.fori_loop(0, seg, fix_body, 0, unroll=8)

    for j in range(nl):
        ga = ga_ref[:, lanes(j)]
        y_ref[:, lanes(j)] = (hn_s[j, 0:seq, :] * (ga * _sigmoid(ga))).astype(y_ref.dtype)
        hl_ref[:, lanes(j)] = hn_s[j, seq - 1:seq, :]


def _rglru(xa, ga, buf8, h0, conv_w, conv_b, wr_bd, br, wi_bd, bi, lam, layer, batch, seq):
    m, d_a = xa.shape
    xa3 = xa.reshape(batch, seq, d_a)
    ga3 = ga.reshape(batch, seq, d_a)
    seg = pl.cdiv(pl.cdiv(seq, SUBLANES), SUBLANES) * SUBLANES
    g_block = max(t for t in range(1, min(seg, 44) + 1) if seg % t == 0)
    cw = max(w for w in range(GATE_BLOCK, d_a + 1, GATE_BLOCK)
             if d_a % w == 0 and (w == GATE_BLOCK or 6 * 4 * SUBLANES * seg * w <= RGLRU_SCRATCH_BYTES))
    nc = d_a // cw
    nl = cw // LANES
    chan = lambda rows: pl.BlockSpec((None, rows, cw), lambda b, c: (b, 0, c))
    par = lambda rows: pl.BlockSpec((None, rows, cw), lambda b, c: (layer, 0, c))
    wspec = pl.BlockSpec((None, cw // GATE_BLOCK, GATE_BLOCK, GATE_BLOCK), lambda b, c: (layer, c, 0, 0))
    y, nbuf, hl = pl.pallas_call(
        functools.partial(_rglru_kernel, seq=seq, seg=seg, g_block=g_block),
        grid=(batch, nc),
        in_specs=[chan(seq), chan(seq), chan(SUBLANES), chan(1), par(CONV_W), par(1),
                  wspec, par(1), wspec, par(1), par(1)],
        out_specs=[chan(seq), chan(SUBLANES), chan(1)],
        out_shape=[jax.ShapeDtypeStruct((batch, seq, d_a), BF16),
                   jax.ShapeDtypeStruct((batch, SUBLANES, d_a), F32),
                   jax.ShapeDtypeStruct((batch, 1, d_a), F32)],
        scratch_shapes=[pltpu.VMEM((nl, SUBLANES * (seg + 1), LANES), F32),
                        pltpu.VMEM((SUBLANES * seg, cw), F32),
                        pltpu.VMEM((SUBLANES * seg, cw), F32),
                        pltpu.VMEM((SUBLANES * seg, cw), F32),
                        pltpu.VMEM((SUBLANES * seg, cw), F32),
                        pltpu.VMEM((nl, SUBLANES * seg, LANES), F32)],
        compiler_params=pltpu.CompilerParams(
            dimension_semantics=("arbitrary", "arbitrary"), vmem_limit_bytes=VMEM_LIMIT_BYTES),
        name="rglru_mixer",
    )(xa3, ga3, buf8, h0, conv_w, conv_b, wr_bd, br, wi_bd, bi, lam)
    return y.reshape(m, d_a), nbuf[:, SUBLANES - (CONV_W - 1):, :], hl[:, 0, :]


def _diff_lambda(lv, lam_init):
    s1 = jnp.sum(lv[0:1, :] * lv[1:2, :], axis=-1, keepdims=True)
    s2 = jnp.sum(lv[2:3, :] * lv[3:4, :], axis=-1, keepdims=True)
    return jnp.exp(s1) - jnp.exp(s2) + lam_init


def _attn_prompt_kernel(lv_ref, q_ref, k_ref, v_ref, gb_ref, bt_ref, g_ref, o_ref,
                        q1t_s, q2t_s, vt_s, acc_s, s_s, *, seq, n_tiles, lam_init):
    gap = LANES - N_META
    lam = _diff_lambda(lv_ref[...], lam_init)

    row = lax.broadcasted_iota(jnp.int32, (2 * DK, LANES), 0)
    zero = jnp.zeros((2 * DK, LANES), BF16)
    pad = jnp.zeros((gap, DV), BF16)
    for j in range(2 * n_tiles + 1):
        cols = slice(LANES * j, LANES * (j + 1))
        if j == 0:
            qb = jnp.concatenate([q_ref[0:N_META, :], pad], axis=0)
            vb = jnp.concatenate([v_ref[0:N_META, :], pad], axis=0)
        else:
            rows = slice(N_META + LANES * (j - 1), N_META + LANES * j)
            qb = q_ref[rows, :]
            vb = v_ref[rows, :]
        qt = qb.T
        q1t_s[:, cols] = jnp.where(row < DK, qt, zero)
        q2t_s[:, cols] = jnp.where(row >= DK, qt, zero)
        vt_s[:, cols] = vb.T

    gcol = g_ref[...] * (1.0 - lam_init)
    meta_pad = jnp.zeros((gap, Q_TILE), BF16)

    tiles = [(0, LANES, 0, seq, 0, N_META)]
    for i in range(n_tiles):
        tiles.append((LANES + Q_TILE * i, Q_TILE, i + 1, Q_TILE * (n_tiles - 1 - i),
                      N_META + Q_TILE * i, Q_TILE))
    q_maps = (q1t_s, q2t_s)
    state = {}

    def key_rows(t, j):
        if j == 0:
            return 0, N_META + (Q_TILE if t else 0)
        return N_META + Q_TILE * j, Q_TILE

    def scores(t, j, slot):
        q_col, n_q, _, b_lo, _, _ = tiles[t]
        r0, nr = key_rows(t, j)
        cms = []
        for m in range(2):
            s = jnp.dot(k_ref[r0:r0 + nr, :], q_maps[m][:, q_col:q_col + n_q],
                        preferred_element_type=F32)
            s = s + bt_ref[b_lo + r0:b_lo + r0 + nr, 0:n_q]
            s_s[slot, m, 0:nr, 0:n_q] = s
            cms.append(jnp.max(s, axis=0, keepdims=True))
        return cms

    def fold(t, j, slot, cms):
        _, n_q, _, _, _, _ = tiles[t]
        r0, nr = key_rows(t, j)
        for m in range(2):
            if j == 0:
                m_new, alpha = cms[m], None
                s = s_s[slot, m, 0:nr, 0:n_q]
                p = jnp.exp2(s - m_new)
                den = jnp.sum(p, axis=0, keepdims=True)
                pb = p.astype(BF16)
                parts = [pb[0:N_META], meta_pad[:, 0:n_q]]
                if nr > N_META:
                    parts.append(pb[N_META:nr])
                pb = jnp.concatenate(parts, axis=0)
                c0, nc = 0, LANES + nr - N_META
            else:
                mx, den = state[t, m]
                m_new = jnp.maximum(mx, cms[m])
                alpha = jnp.exp2(mx - m_new)
                p = jnp.exp2(s_s[slot, m, 0:nr, 0:n_q] - m_new)
                den = alpha * den + jnp.sum(p, axis=0, keepdims=True)
                pb = p.astype(BF16)
                c0, nc = LANES + Q_TILE * j, Q_TILE
            state[t, m] = (m_new, den)
            o = jnp.dot(vt_s[:, c0:c0 + nc], pb, preferred_element_type=F32)
            if alpha is None:
                acc_s[t, m, :, 0:n_q] = o
            else:
                acc_s[t, m, :, 0:n_q] = alpha * acc_s[t, m, :, 0:n_q] + o

    def finish(t):
        _, n_q, _, _, q_lo, out_rows = tiles[t]
        outs = [acc_s[t, m, :, 0:n_q] * (1.0 / state[t, m][1]) for m in range(2)]
        ot = outs[0] - lam * outs[1]
        ms = jnp.mean(ot * ot, axis=0, keepdims=True)
        on = (ot * lax.rsqrt(ms + EPS)) * gcol
        o_rows = on.T[0:out_rows, :]
        gb = gb_ref[q_lo:q_lo + out_rows, :].astype(F32)
        o_ref[q_lo:q_lo + out_rows, :] = (o_rows * (gb * _sigmoid(gb))).astype(o_ref.dtype)

    jobs = [(0, 0)] + [(t, j) for _, t, j in sorted(
        ((j + 0.5) / tiles[t][2], t, j) for t in range(1, len(tiles)) for j in range(tiles[t][2]))]
    cms_next = scores(*jobs[0], 0)
    for n, (t, j) in enumerate(jobs):
        cms = cms_next
        if n + 1 < len(jobs):
            cms_next = scores(*jobs[n + 1], (n + 1) % 2)
        fold(t, j, n % 2, cms)
        if j == max(tiles[t][2] - 1, 0):
            finish(t)


def _attn_prompt(q, kb, vb, gb, slab, lam_vec, subln_col, layer, lam_init, batch, seq):
    m, width = q.shape
    n_tiles = (seq - N_META) // Q_TILE
    assert N_META + n_tiles * Q_TILE == seq
    tp = (2 * n_tiles + 1) * LANES
    to3 = lambda a: a.reshape(batch, seq, width)
    head = pl.BlockSpec((None, seq, DV), lambda h, b: (b, 0, h))
    out = pl.pallas_call(
        functools.partial(_attn_prompt_kernel, seq=seq, n_tiles=n_tiles, lam_init=lam_init),
        grid=(N_HEADS_B, batch),
        in_specs=[pl.BlockSpec((None, 4, DK), lambda h, b: (layer, 0, 0)),
                  head, head, head, head,
                  pl.BlockSpec((None, seq + N_META, Q_TILE), lambda h, b: (h, 0, 0)),
                  pl.BlockSpec((None, DV, 1), lambda h, b: (layer, 0, 0))],
        out_specs=head,
        out_shape=jax.ShapeDtypeStruct((batch, seq, width), BF16),
        scratch_shapes=[pltpu.VMEM((2 * DK, tp), BF16),
                        pltpu.VMEM((2 * DK, tp), BF16),
                        pltpu.VMEM((DV, tp), BF16),
                        pltpu.VMEM((n_tiles + 1, 2, DV, Q_TILE), F32),
                        pltpu.VMEM((2, 2, N_META + Q_TILE, Q_TILE), F32)],
        compiler_params=pltpu.CompilerParams(
            dimension_semantics=("arbitrary", "arbitrary"), vmem_limit_bytes=VMEM_LIMIT_BYTES),
        name="diff_attention_prompt",
    )(lam_vec, to3(q), to3(kb), to3(vb), to3(gb), slab, subln_col)
    return out.reshape(m, width)


def _attn_sample_kernel(lv_ref, q_ref, kc_ref, vc_ref, kn_ref, vn_ref, gb_ref, bs_ref, g_ref, o_ref,
                        kb_s, vb_s, s_s, *, past, dec_seq, lam_init):
    lam = _diff_lambda(lv_ref[...], lam_init)
    width = past + LANES
    lanes = lambda h: slice(h * DV, (h + 1) * DV)
    lane = lax.broadcasted_iota(jnp.int32, (dec_seq, DV), 1)
    zero = jnp.zeros((dec_seq, DV), BF16)
    tail = jnp.zeros((LANES - dec_seq, DV), BF16)

    def scores(h, slot):
        for dst, cache, new in ((kb_s, kc_ref, kn_ref), (vb_s, vc_ref, vn_ref)):
            dst[slot, 0:past, :] = cache[pl.ds(h, past, stride=N_HEADS_B), :].astype(BF16)
            dst[slot, past:past + dec_seq, :] = new[:, lanes(h)]
            dst[slot, past + dec_seq:width, :] = tail
        qv = q_ref[:, lanes(h)]
        for m, qm in enumerate((jnp.where(lane < DK, qv, zero), jnp.where(lane >= DK, qv, zero))):
            s = lax.dot_general(qm, kb_s[slot], (((1,), (1,)), ((), ())), preferred_element_type=F32)
            s_s[slot, m] = s + bs_ref[h]

    def fold(h, slot):
        probs = []
        for m in range(2):
            s = s_s[slot, m]
            mx = jnp.max(s, axis=-1, keepdims=True)
            p = jnp.exp2(s - mx)
            probs.append(p * (1.0 / jnp.sum(p, axis=-1, keepdims=True)))
        pd = (probs[0] - lam * probs[1]).astype(BF16)
        o = jnp.dot(pd, vb_s[slot], preferred_element_type=F32)
        ms = jnp.mean(o * o, axis=-1, keepdims=True)
        on = ((o * lax.rsqrt(ms + EPS)) * g_ref[...]) * (1.0 - lam_init)
        gb = gb_ref[:, lanes(h)].astype(F32)
        o_ref[:, lanes(h)] = (on * (gb * _sigmoid(gb))).astype(o_ref.dtype)

    scores(0, 0)
    for h in range(N_HEADS_B):
        if h + 1 < N_HEADS_B:
            scores(h + 1, (h + 1) % 2)
        fold(h, h % 2)


def _attn_sample(q, kb, vb, gb, cache_k, cache_v, slab, lam_vec, subln_row, layer, lam_init,
                 batch, dec_seq):
    m, width = q.shape
    depth, _, past = cache_k.shape[:3]
    ck = cache_k.reshape(depth, batch, past * N_HEADS_B, DV)
    cv = cache_v.reshape(depth, batch, past * N_HEADS_B, DV)
    rows = pl.BlockSpec((dec_seq, width), lambda b: (b, 0))
    cache = pl.BlockSpec((None, None, past * N_HEADS_B, DV), lambda b: (layer, b, 0, 0))
    return pl.pallas_call(
        functools.partial(_attn_sample_kernel, past=past, dec_seq=dec_seq, lam_init=lam_init),
        grid=(batch,),
        in_specs=[pl.BlockSpec((None, 4, DK), lambda b: (layer, 0, 0)),
                  rows, cache, cache, rows, rows, rows,
                  pl.BlockSpec((N_HEADS_B, dec_seq, past + LANES), lambda b: (0, 0, 0)),
                  pl.BlockSpec((None, 1, DV), lambda b: (layer, 0, 0))],
        out_specs=rows,
        out_shape=jax.ShapeDtypeStruct((m, width), BF16),
        scratch_shapes=[pltpu.VMEM((2, past + LANES, DV), BF16),
                        pltpu.VMEM((2, past + LANES, DV), BF16),
                        pltpu.VMEM((2, 2, dec_seq, past + LANES), F32)],
        compiler_params=pltpu.CompilerParams(
            dimension_semantics=("arbitrary",), vmem_limit_bytes=VMEM_LIMIT_BYTES),
        name="diff_attention_sample",
    )(lam_vec, q, ck, cv, kb, vb, gb, slab, subln_row)


def _outproj_kernel(ya_ref, yb_ref, x_ref, w_ref, g_ref, o_ref, y_s, *, col_chunk):
    d_a = ya_ref.shape[-1]
    d = o_ref.shape[-1]
    for c in range(0, d, col_chunk):
        y_s[:, c:c + col_chunk] = (
            jnp.dot(ya_ref[...], w_ref[0:d_a, c:c + col_chunk], preferred_element_type=F32)
            + jnp.dot(yb_ref[...], w_ref[d_a:, c:c + col_chunk], preferred_element_type=F32))
    y = y_s[...]
    ms = jnp.mean(y * y, axis=-1, keepdims=True)
    o_ref[...] = x_ref[...] + (y * lax.rsqrt(ms + EPS)) * g_ref[...]


def _outproj(ya, yb, x2d, w_out_b, post_g, layer):
    m, d = x2d.shape
    tm = _row_tile(m, 384)
    half = pl.BlockSpec((tm, d // 2), lambda i: (i, 0))
    full = pl.BlockSpec((tm, d), lambda i: (i, 0))
    return pl.pallas_call(
        functools.partial(_outproj_kernel, col_chunk=512),
        grid=(m // tm,),
        in_specs=[half, half, full,
                  pl.BlockSpec((None, d, d), lambda i: (layer, 0, 0)),
                  pl.BlockSpec((None, 1, d), lambda i: (layer, 0, 0))],
        out_specs=full,
        out_shape=jax.ShapeDtypeStruct((m, d), F32),
        scratch_shapes=[pltpu.VMEM((tm, d), F32)],
        compiler_params=pltpu.CompilerParams(
            dimension_semantics=("arbitrary",), vmem_limit_bytes=VMEM_LIMIT_BYTES),
        name="out_projection",
    )(ya, yb, x2d, w_out_b, post_g)


def _outproj_main_kernel(ya_ref, yb_ref, x_ref, w_ref, g_ref, o_ref, y_s, *, col_chunk):
    _outproj_kernel(ya_ref.at[0], yb_ref.at[0], x_ref.at[0], w_ref, g_ref, o_ref, y_s,
                    col_chunk=col_chunk)


def _outproj_main(ya, yb, x2d, w_out_b, post_g, layer, batch, seq):
    m, d = x2d.shape
    main = seq - N_META
    tm = _row_tile(main, 512)
    assert tm % N_META == 0
    row0 = lambda j: (j * (tm // N_META) + 1) * N_META
    win = lambda w: pl.BlockSpec((pl.Element(1), pl.Element(tm), pl.Element(w)),
                                 lambda b, j: (b, row0(j), 0))
    return pl.pallas_call(
        functools.partial(_outproj_main_kernel, col_chunk=512),
        grid=(batch, main // tm),
        in_specs=[win(d // 2), win(d // 2), win(d),
                  pl.BlockSpec((None, d, d), lambda b, j: (layer, 0, 0)),
                  pl.BlockSpec((None, 1, d), lambda b, j: (layer, 0, 0))],
        out_specs=pl.BlockSpec((None, tm, d), lambda b, j: (b, j, 0)),
        out_shape=jax.ShapeDtypeStruct((batch, main, d), F32),
        scratch_shapes=[pltpu.VMEM((tm, d), F32)],
        compiler_params=pltpu.CompilerParams(
            dimension_semantics=("arbitrary", "arbitrary"), vmem_limit_bytes=VMEM_LIMIT_BYTES),
        name="out_projection_main",
    )(ya.reshape(batch, seq, d // 2), yb.reshape(batch, seq, d // 2), x2d.reshape(batch, seq, d),
      w_out_b, post_g)


def _block_diag_gates(w):
    depth, nb, bs, _ = w.shape
    per = GATE_BLOCK // bs
    w5 = w.reshape(depth, nb // per, per, bs, bs)
    eye = jnp.eye(per, dtype=w.dtype)
    bd = jnp.einsum('lgaij,ab->lgaibj', w5, eye)
    return bd.reshape(depth, nb // per, GATE_BLOCK, GATE_BLOCK).astype(BF16)


def kernel(x_prompt, x_sample, cache_k, cache_v, state_conv, state_rglru, meta, rel_bias, pre_g, post_g,
           w_in, conv_w, conv_b, gate_r_w, gate_r_b, gate_i_w, gate_i_b, rglru_lam, lam_q1, lam_k1,
           lam_q2, lam_k2, subln_g, w_out):
    batch, seq0, d = x_prompt.shape
    dec_batch, dec_seq, _ = x_sample.shape
    depth = w_in.shape[0]
    past = cache_k.shape[2]
    d_a = d // 2
    seq = seq0 + N_META
    n_tiles = seq0 // Q_TILE

    hp = jnp.concatenate(
        [jnp.broadcast_to(meta.astype(x_prompt.dtype)[None], (batch, N_META, d)), x_prompt],
        axis=1).reshape(batch * seq, d)
    hs = x_sample.reshape(dec_batch * dec_seq, d)

    w_in_b = w_in.astype(BF16)
    w_out_b = w_out.astype(BF16)
    wr_bd = _block_diag_gates(gate_r_w)
    wi_bd = _block_diag_gates(gate_i_w)
    pre_g3 = pre_g.reshape(depth, 1, d)
    post_g3 = post_g.reshape(depth, 1, d)
    conv_b3 = conv_b.reshape(depth, 1, d_a)
    br3 = gate_r_b.reshape(depth, 1, d_a)
    bi3 = gate_i_b.reshape(depth, 1, d_a)
    lam3 = rglru_lam.reshape(depth, 1, d_a)
    lam_vec = jnp.stack([lam_q1, lam_k1, lam_q2, lam_k2], axis=1)
    subln_col = subln_g.reshape(depth, DV, 1)
    subln_row = subln_g.reshape(depth, 1, DV)

    slab_p, slab_s = _bias_slabs(rel_bias, n_tiles, past, dec_seq)

    zero_buf = jnp.zeros((batch, SUBLANES, d_a), F32)
    zero_h = jnp.zeros((batch, 1, d_a), F32)
    pad_rows = SUBLANES - (CONV_W - 1)
    state_conv8 = jnp.pad(state_conv, ((0, 0), (0, 0), (pad_rows, 0), (0, 0)))

    kv_p = None
    kv_s = None
    cp_l, rp_l, cs_l, rs_l = [], [], [], []
    for l in range(depth):
        lam_init = 0.8 - 0.6 * math.exp(-0.3 * l)
        xa, ga, q, kb, vb, gb, k5, v5 = _inproj(hp, pre_g3, w_in_b, l, kv_p, depth)
        kv_p = (k5, v5)
        ya, cp, rp = _rglru(xa, ga, zero_buf, zero_h, conv_w, conv_b3, wr_bd, br3, wi_bd, bi3, lam3,
                            l, batch, seq)
        yb = _attn_prompt(q, kb, vb, gb, slab_p, lam_vec, subln_col, l, lam_init, batch, seq)
        if l + 1 < depth:
            hp = _outproj(ya, yb, hp, w_out_b, post_g3, l)
        else:
            y_prompt = _outproj_main(ya, yb, hp, w_out_b, post_g3, l, batch, seq)
        cp_l.append(cp)
        rp_l.append(rp)
        xa, ga, q, kb, vb, gb, k5, v5 = _inproj(hs, pre_g3, w_in_b, l, kv_s, depth)
        kv_s = (k5, v5)
        ya, cs, rs = _rglru(xa, ga, state_conv8[l], state_rglru[l][:, None, :], conv_w, conv_b3,
                            wr_bd, br3, wi_bd, bi3, lam3, l, dec_batch, dec_seq)
        yb = _attn_sample(q, kb, vb, gb, cache_k, cache_v, slab_s, lam_vec, subln_row, l, lam_init,
                          dec_batch, dec_seq)
        hs = _outproj(ya, yb, hs, w_out_b, post_g3, l)
        cs_l.append(cs)
        rs_l.append(rs)

    y_sample = hs.reshape(dec_batch, dec_seq, d)
    k_prompt = kv_p[0].reshape(depth, batch, seq, N_HEADS_B, 2 * DK)
    v_prompt = kv_p[1].reshape(depth, batch, seq, N_HEADS_B, DV)
    k_sample = kv_s[0].reshape(depth, dec_batch, dec_seq, N_HEADS_B, 2 * DK)
    v_sample = kv_s[1].reshape(depth, dec_batch, dec_seq, N_HEADS_B, DV)
    return (y_prompt, y_sample, k_prompt, v_prompt, jnp.stack(cp_l), jnp.stack(rp_l),
            k_sample, v_sample, jnp.stack(cs_l), jnp.stack(rs_l))
```

```python
import functools
import math

import jax
import jax.numpy as jnp
from jax import lax
from jax.experimental import pallas as pl
from jax.experimental.pallas import tpu as pltpu

F32 = jnp.float32
BF16 = jnp.bfloat16

CHUNK = 64
N_META = 16
N_BLOCKS_A = 16
CONV_W = 4
RG_C = 8.0
N_HEADS_B = 8
DK = 64
DV = 2 * DK
NUM_BUCKETS = 32
REL_MAX_DIST = 1024
EPS = 1e-6

LANES = 128
SUBLANES = 8
BF16_ROWS = 16
VMEM_LIMIT_BYTES = 56 * 1024 * 1024

LOG2E = math.log2(math.e)
Q_TILE = 256
FAR_REL = math.ceil((NUM_BUCKETS // 4) * (REL_MAX_DIST / (NUM_BUCKETS // 4))
                    ** ((NUM_BUCKETS // 4 - 1) / (NUM_BUCKETS // 4))) + CHUNK
GATE_BLOCK = 256
RGLRU_SCRATCH_BYTES = 16 * 1024 * 1024


def _sigmoid(x):
    return 1.0 / (1.0 + jnp.exp(-x))


def _row_tile(m, cap=512):
    best = None
    for t in range(BF16_ROWS, min(m, cap) + 1, BF16_ROWS):
        if m % t == 0:
            best = t
    assert best is not None, m
    return best


def _bias_lookup(rel, tab_row):
    half = NUM_BUCKETS // 2
    max_exact = half // 2
    ret = jnp.where(rel > 0, half, 0).astype(jnp.int32)
    n = jnp.abs(rel)
    nf = jnp.maximum(n, 1).astype(F32)
    large = max_exact + (jnp.log(nf / max_exact) / math.log(REL_MAX_DIST / max_exact)
                         * (half - max_exact)).astype(jnp.int32)
    large = jnp.minimum(large, half - 1)
    bucket = ret + jnp.where(n < max_exact, n, large)
    tab = jnp.broadcast_to(tab_row, (rel.shape[0], LANES))
    return jnp.concatenate(
        [jnp.take_along_axis(tab, bucket[:, c:c + LANES], axis=1) for c in range(0, rel.shape[1], LANES)],
        axis=1)


def _prompt_slab_kernel(tab_ref, out_ref, *, n_tiles):
    seq = Q_TILE * n_tiles + N_META
    diag0 = Q_TILE * (n_tiles - 1) + N_META
    far_bucket = NUM_BUCKETS // 2 - 1
    tab_row = tab_ref[...] - tab_ref[:, far_bucket:far_bucket + 1]
    r0 = 0
    while r0 < seq:
        blk = min(LANES, seq - r0)
        r = lax.broadcasted_iota(jnp.int32, (blk, Q_TILE), 0) + r0
        c = lax.broadcasted_iota(jnp.int32, (blk, Q_TILE), 1)
        val = _bias_lookup(r - c - diag0, tab_row) * LOG2E
        rp = r - diag0
        masked = jnp.logical_and(rp >= 0, (rp // CHUNK) > (c // CHUNK))
        out_ref[r0:r0 + blk, :] = jnp.where(masked, -jnp.inf, val)
        r0 += blk
    k = lax.broadcasted_iota(jnp.int32, (N_META, Q_TILE), 0)
    c = lax.broadcasted_iota(jnp.int32, (N_META, Q_TILE), 1)
    out_ref[seq:seq + N_META, :] = _bias_lookup(k - c, tab_row) * LOG2E


def _sample_slab_kernel(tab_ref, out_ref, *, past, dec_seq):
    width = out_ref.shape[-1]
    for c0 in range(0, width, 512):
        w = min(512, width - c0)
        t = lax.broadcasted_iota(jnp.int32, (dec_seq, w), 0)
        k = lax.broadcasted_iota(jnp.int32, (dec_seq, w), 1) + c0
        val = _bias_lookup(k - past - t, tab_ref[...]) * LOG2E
        out_ref[:, c0:c0 + w] = jnp.where(k >= past + dec_seq, -jnp.inf, val)


def _bias_slabs(rel_bias, n_tiles, past, dec_seq):
    rows = Q_TILE * n_tiles + 2 * N_META
    tab = jnp.pad(rel_bias.T, ((0, 0), (0, LANES - NUM_BUCKETS)))[:, None, :]
    tab_spec = pl.BlockSpec((None, 1, LANES), lambda h: (h, 0, 0))
    prompt = pl.pallas_call(
        functools.partial(_prompt_slab_kernel, n_tiles=n_tiles),
        grid=(N_HEADS_B,),
        in_specs=[tab_spec],
        out_specs=pl.BlockSpec((None, rows, Q_TILE), lambda h: (h, 0, 0)),
        out_shape=jax.ShapeDtypeStruct((N_HEADS_B, rows, Q_TILE), F32),
        name="prompt_bias_slab",
    )(tab)
    width = past + LANES
    sample = pl.pallas_call(
        functools.partial(_sample_slab_kernel, past=past, dec_seq=dec_seq),
        grid=(N_HEADS_B,),
        in_specs=[tab_spec],
        out_specs=pl.BlockSpec((None, dec_seq, width), lambda h: (h, 0, 0)),
        out_shape=jax.ShapeDtypeStruct((N_HEADS_B, dec_seq, width), F32),
        name="sample_bias_slab",
    )(tab)
    return prompt, sample


def _inproj_kernel(x_ref, g_ref, w_ref, *rest, col_chunk):
    xa_ref, ga_ref, q_ref, kb_ref, vb_ref, gb_ref, k5_ref, v5_ref, u_s = rest[-9:]
    tm = x_ref.shape[0]

    def normalise(lo, n):
        x = x_ref[lo:lo + n, :]
        ms = jnp.mean(x * x, axis=-1, keepdims=True)
        u_s[lo:lo + n, :] = ((x * lax.rsqrt(ms + EPS)) * g_ref[...]).astype(BF16)

    chunks = []
    col = 0
    for ref, scale, ref5 in ((xa_ref, None, None), (ga_ref, None, None),
                             (q_ref, (DK ** -0.5) * LOG2E, None),
                             (kb_ref, None, k5_ref), (vb_ref, None, v5_ref), (gb_ref, None, None)):
        for c in range(0, ref.shape[-1], col_chunk):
            chunks.append((ref, scale, ref5, col + c, c))
        col += ref.shape[-1]

    def project(lo, n, chunk):
        ref, scale, ref5, wcol, c = chunk
        acc = jnp.dot(u_s[lo:lo + n, :], w_ref[:, wcol:wcol + col_chunk], preferred_element_type=F32)
        if scale is not None:
            acc = acc * scale
        ref[lo:lo + n, c:c + col_chunk] = acc.astype(ref.dtype)
        if ref5 is not None:
            for j in range(col_chunk // DV):
                head = c // DV + j
                ref5[pl.ds(lo * N_HEADS_B + head, n, stride=N_HEADS_B), :] = acc[:, j * DV:(j + 1) * DV]

    normalise(0, tm)
    for chunk in chunks:
        project(0, tm, chunk)


def _inproj(x2d, pre_g, w_in_b, layer, kv_stacks, depth):
    m, d = x2d.shape
    d_a = d // 2
    width = N_HEADS_B * DV
    in_cols = w_in_b.shape[-1]
    tm = _row_tile(m, 384)
    row = lambda w: pl.BlockSpec((tm, w), lambda i: (i, 0))
    stack = pl.BlockSpec((None, tm * N_HEADS_B, DV), lambda i: (layer, i, 0))
    in_specs = [
        row(d),
        pl.BlockSpec((None, 1, d), lambda i: (layer, 0, 0)),
        pl.BlockSpec((None, d, in_cols), lambda i: (layer, 0, 0), pipeline_mode=pl.Buffered(1)),
    ]
    args = [x2d, pre_g, w_in_b]
    aliases = {}
    if kv_stacks is not None:
        in_specs += [pl.BlockSpec(memory_space=pl.ANY)] * 2
        args += list(kv_stacks)
        aliases = {3: 6, 4: 7}
    out_shape = [
        jax.ShapeDtypeStruct((m, d_a), F32),
        jax.ShapeDtypeStruct((m, d_a), F32),
        jax.ShapeDtypeStruct((m, width), BF16),
        jax.ShapeDtypeStruct((m, width), BF16),
        jax.ShapeDtypeStruct((m, width), BF16),
        jax.ShapeDtypeStruct((m, width), BF16),
        jax.ShapeDtypeStruct((depth, m * N_HEADS_B, DV), F32),
        jax.ShapeDtypeStruct((depth, m * N_HEADS_B, DV), F32),
    ]
    out_specs = [row(d_a), row(d_a), row(width), row(width), row(width), row(width), stack, stack]
    return pl.pallas_call(
        functools.partial(_inproj_kernel, col_chunk=512),
        grid=(m // tm,),
        in_specs=in_specs,
        out_specs=out_specs,
        out_shape=out_shape,
        scratch_shapes=[pltpu.VMEM((tm, d), BF16)],
        input_output_aliases=aliases,
        compiler_params=pltpu.CompilerParams(
            dimension_semantics=("arbitrary",), vmem_limit_bytes=VMEM_LIMIT_BYTES),
        name="in_projection",
    )(*args)


def _rglru_kernel(xa_ref, ga_ref, buf_ref, h0_ref, cw_ref, cb_ref, wr_ref, br_ref, wi_ref, bi_ref,
                  lam_ref, y_ref, nbuf_ref, hl_ref, xp_s, a_s, b_s, hs_s, as_s, hn_s, *, seq, seg, g_block):
    c = xa_ref.shape[-1]
    nl = c // LANES
    lanes = lambda j: slice(j * LANES, (j + 1) * LANES)
    pad = SUBLANES * seg - seq
    for j in range(nl):
        xp_s[j, 0:SUBLANES, :] = buf_ref[:, lanes(j)]
        xp_s[j, SUBLANES:SUBLANES + seq, :] = xa_ref[:, lanes(j)]
        if pad:
            xp_s[j, SUBLANES + seq:, :] = jnp.zeros((pad, LANES), F32)
        nbuf_ref[:, lanes(j)] = xp_s[j, seq:seq + SUBLANES, :]

    neg = -lam_ref[...]
    softplus = jnp.maximum(neg, 0.0) + jnp.log1p(jnp.exp(-jnp.abs(neg)))
    cneg = -RG_C * softplus
    cb = cb_ref[...]
    taps = [cw_ref[j:j + 1, :] for j in range(CONV_W)]

    cache = {}

    def rows_at(g):
        if g not in cache:
            cache[g] = jnp.concatenate(
                [xp_s[j, pl.ds(SUBLANES + g, SUBLANES, stride=seg), :] for j in range(nl)], axis=1)
        return cache[g]

    for g0 in range(0, seg, g_block):
        pieces = []
        for g in range(g0, g0 + g_block):
            xc = cb
            for j in range(CONV_W):
                xc = xc + rows_at(g - (CONV_W - 1) + j) * taps[j]
            pieces.append(xc)
        xc = jnp.concatenate(pieces, axis=0)
        xcb = xc.astype(BF16)

        def gate(w_ref, b_ref):
            z = [jnp.dot(xcb[:, k * GATE_BLOCK:(k + 1) * GATE_BLOCK], w_ref[k],
                         preferred_element_type=F32) for k in range(c // GATE_BLOCK)]
            return _sigmoid(jnp.concatenate(z, axis=1) + b_ref[...])

        r = gate(wr_ref, br_ref)
        ig = gate(wi_ref, bi_ref)
        log_a = r * cneg
        a = jnp.exp(log_a)
        lo, hi = g0 * SUBLANES, (g0 + g_block) * SUBLANES
        a_s[lo:hi, :] = a
        e = -jnp.tanh(log_a) * (a * a + 1.0)
        root = jnp.where(e > 0.0, e * lax.rsqrt(e), 0.0)
        b_s[lo:hi, :] = root * (ig * xc)

    rowid = lax.broadcasted_iota(jnp.int32, (SUBLANES, c), 0)
    first = rowid == 0

    def scan_body(g, carry):
        h, acc = carry
        r = pl.multiple_of(g * SUBLANES, SUBLANES)
        a = a_s[pl.ds(r, SUBLANES), :]
        h = a * h + b_s[pl.ds(r, SUBLANES), :]
        acc = a * acc
        hs_s[pl.ds(r, SUBLANES), :] = h
        as_s[pl.ds(r, SUBLANES), :] = acc
        return h, acc

    h_init = jnp.where(first, jnp.broadcast_to(h0_ref[...], (SUBLANES, c)), 0.0)
    end_h, end_a = lax.fori_loop(0, seg, scan_body, (h_init, jnp.ones((SUBLANES, c), F32)), unroll=8)

    d = jnp.zeros((SUBLANES, c), F32)
    for _ in range(SUBLANES - 1):
        d = jnp.where(first, 0.0, pltpu.roll(end_h + end_a * d, 1, 0))

    def fix_body(g, carry):
        r = pl.multiple_of(g * SUBLANES, SUBLANES)
        h = hs_s[pl.ds(r, SUBLANES), :] + as_s[pl.ds(r, SUBLANES), :] * d
        for j in range(nl):
            hn_s[j, pl.ds(g, SUBLANES, stride=seg), :] = h[:, lanes(j)]
        return carry

    lax.fori_loop(0, seg, fix_body, 0, unroll=8)

    for j in range(nl):
        ga = ga_ref[:, lanes(j)]
        y_ref[:, lanes(j)] = (hn_s[j, 0:seq, :] * (ga * _sigmoid(ga))).astype(y_ref.dtype)
        hl_ref[:, lanes(j)] = hn_s[j, seq - 1:seq, :]


def _rglru(xa, ga, buf8, h0, conv_w, conv_b, wr_bd, br, wi_bd, bi, lam, layer, batch, seq):
    m, d_a = xa.shape
    xa3 = xa.reshape(batch, seq, d_a)
    ga3 = ga.reshape(batch, seq, d_a)
    seg = pl.cdiv(pl.cdiv(seq, SUBLANES), SUBLANES) * SUBLANES
    g_block = max(t for t in range(1, min(seg, 44) + 1) if seg % t == 0)
    cw = max(w for w in range(GATE_BLOCK, d_a + 1, GATE_BLOCK)
             if d_a % w == 0 and (w == GATE_BLOCK or 6 * 4 * SUBLANES * seg * w <= RGLRU_SCRATCH_BYTES))
    nc = d_a // cw
    nl = cw // LANES
    chan = lambda rows: pl.BlockSpec((None, rows, cw), lambda b, c: (b, 0, c))
    par = lambda rows: pl.BlockSpec((None, rows, cw), lambda b, c: (layer, 0, c))
    wspec = pl.BlockSpec((None, cw // GATE_BLOCK, GATE_BLOCK, GATE_BLOCK), lambda b, c: (layer, c, 0, 0))
    y, nbuf, hl = pl.pallas_call(
        functools.partial(_rglru_kernel, seq=seq, seg=seg, g_block=g_block),
        grid=(batch, nc),
        in_specs=[chan(seq), chan(seq), chan(SUBLANES), chan(1), par(CONV_W), par(1),
                  wspec, par(1), wspec, par(1), par(1)],
        out_specs=[chan(seq), chan(SUBLANES), chan(1)],
        out_shape=[jax.ShapeDtypeStruct((batch, seq, d_a), BF16),
                   jax.ShapeDtypeStruct((batch, SUBLANES, d_a), F32),
                   jax.ShapeDtypeStruct((batch, 1, d_a), F32)],
        scratch_shapes=[pltpu.VMEM((nl, SUBLANES * (seg + 1), LANES), F32),
                        pltpu.VMEM((SUBLANES * seg, cw), F32),
                        pltpu.VMEM((SUBLANES * seg, cw), F32),
                        pltpu.VMEM((SUBLANES * seg, cw), F32),
                        pltpu.VMEM((SUBLANES * seg, cw), F32),
                        pltpu.VMEM((nl, SUBLANES * seg, LANES), F32)],
        compiler_params=pltpu.CompilerParams(
            dimension_semantics=("arbitrary", "arbitrary"), vmem_limit_bytes=VMEM_LIMIT_BYTES),
        name="rglru_mixer",
    )(xa3, ga3, buf8, h0, conv_w, conv_b, wr_bd, br, wi_bd, bi, lam)
    return y.reshape(m, d_a), nbuf[:, SUBLANES - (CONV_W - 1):, :], hl[:, 0, :]


def _diff_lambda(lv, lam_init):
    s1 = jnp.sum(lv[0:1, :] * lv[1:2, :], axis=-1, keepdims=True)
    s2 = jnp.sum(lv[2:3, :] * lv[3:4, :], axis=-1, keepdims=True)
    return jnp.exp(s1) - jnp.exp(s2) + lam_init


def _attn_prompt_kernel(lv_ref, q_ref, k_ref, v_ref, gb_ref, bt_ref, g_ref, o_ref,
                        q1t_s, q2t_s, vt_s, acc_s, s_s, *, seq, n_tiles, lam_init):
    gap = LANES - N_META
    lam = _diff_lambda(lv_ref[...], lam_init)

    row = lax.broadcasted_iota(jnp.int32, (2 * DK, LANES), 0)
    zero = jnp.zeros((2 * DK, LANES), BF16)
    pad = jnp.zeros((gap, DV), BF16)
    for j in range(2 * n_tiles + 1):
        cols = slice(LANES * j, LANES * (j + 1))
        if j == 0:
            qb = jnp.concatenate([q_ref[0:N_META, :], pad], axis=0)
            vb = jnp.concatenate([v_ref[0:N_META, :], pad], axis=0)
        else:
            rows = slice(N_META + LANES * (j - 1), N_META + LANES * j)
            qb = q_ref[rows, :]
            vb = v_ref[rows, :]
        qt = qb.T
        q1t_s[:, cols] = jnp.where(row < DK, qt, zero)
        q2t_s[:, cols] = jnp.where(row >= DK, qt, zero)
        vt_s[:, cols] = vb.T

    gcol = g_ref[...] * (1.0 - lam_init)
    meta_pad = jnp.zeros((gap, Q_TILE), BF16)

    tiles = [(0, LANES, 0, seq, 0, N_META)]
    for i in range(n_tiles):
        tiles.append((LANES + Q_TILE * i, Q_TILE, i + 1, Q_TILE * (n_tiles - 1 - i),
                      N_META + Q_TILE * i, Q_TILE))
    q_maps = (q1t_s, q2t_s)
    state = {}

    def key_rows(t, j):
        if j == 0:
            return 0, N_META + (Q_TILE if t else 0)
        return N_META + Q_TILE * j, Q_TILE

    def scores(t, j, slot):
        q_col, n_q, _, b_lo, _, _ = tiles[t]
        r0, nr = key_rows(t, j)
        far = t > 0 and r0 + nr - 1 - tiles[t][4] <= -FAR_REL
        cms = []
        for m in range(2):
            s = jnp.dot(k_ref[r0:r0 + nr, :], q_maps[m][:, q_col:q_col + n_q],
                        preferred_element_type=F32)
            if not far:
                s = s + bt_ref[b_lo + r0:b_lo + r0 + nr, 0:n_q]
            s_s[slot, m, 0:nr, 0:n_q] = s
            cms.append(jnp.max(s, axis=0, keepdims=True))
        return cms

    def fold(t, j, slot, cms):
        _, n_q, _, _, _, _ = tiles[t]
        r0, nr = key_rows(t, j)
        ops = []
        for m in range(2):
            if j == 0:
                m_new, alpha = cms[m], None
                s = s_s[slot, m, 0:nr, 0:n_q]
                p = jnp.exp2(s - m_new)
                den = jnp.sum(p, axis=0, keepdims=True)
                pb = p.astype(BF16)
                parts = [pb[0:N_META], meta_pad[:, 0:n_q]]
                if nr > N_META:
                    parts.append(pb[N_META:nr])
                pb = jnp.concatenate(parts, axis=0)
                c0, nc = 0, LANES + nr - N_META
            else:
                mx, den = state[t, m]
                m_new = jnp.maximum(mx, cms[m])
                alpha = jnp.exp2(mx - m_new)
                p = jnp.exp2(s_s[slot, m, 0:nr, 0:n_q] - m_new)
                den = alpha * den + jnp.sum(p, axis=0, keepdims=True)
                pb = p.astype(BF16)
                c0, nc = LANES + Q_TILE * j, Q_TILE
            state[t, m] = (m_new, den)
            ops.append((alpha, pb, c0, nc))
        for m, (alpha, pb, c0, nc) in enumerate(ops):
            o = jnp.dot(vt_s[:, c0:c0 + nc], pb, preferred_element_type=F32)
            if alpha is None:
                acc_s[t, m, :, 0:n_q] = o
            else:
                acc_s[t, m, :, 0:n_q] = alpha * acc_s[t, m, :, 0:n_q] + o

    def finish(t):
        _, n_q, _, _, q_lo, out_rows = tiles[t]
        outs = [acc_s[t, m, :, 0:n_q] * (1.0 / state[t, m][1]) for m in range(2)]
        ot = outs[0] - lam * outs[1]
        ms = jnp.mean(ot * ot, axis=0, keepdims=True)
        on = (ot * lax.rsqrt(ms + EPS)) * gcol
        o_rows = on.T[0:out_rows, :]
        gb = gb_ref[q_lo:q_lo + out_rows, :].astype(F32)
        o_ref[q_lo:q_lo + out_rows, :] = (o_rows * (gb * _sigmoid(gb))).astype(o_ref.dtype)

    jobs = [(0, 0)] + [(t, j) for _, t, j in sorted(
        ((j + 0.5) / tiles[t][2], t, j) for t in range(1, len(tiles)) for j in range(tiles[t][2]))]
    cms_next = scores(*jobs[0], 0)
    for n, (t, j) in enumerate(jobs):
        cms = cms_next
        if n + 1 < len(jobs):
            cms_next = scores(*jobs[n + 1], (n + 1) % 2)
        fold(t, j, n % 2, cms)
        if j == max(tiles[t][2] - 1, 0):
            finish(t)


def _attn_prompt(q, kb, vb, gb, slab, lam_vec, subln_col, layer, lam_init, batch, seq):
    m, width = q.shape
    n_tiles = (seq - N_META) // Q_TILE
    assert N_META + n_tiles * Q_TILE == seq
    tp = (2 * n_tiles + 1) * LANES
    to3 = lambda a: a.reshape(batch, seq, width)
    head = pl.BlockSpec((None, seq, DV), lambda h, b: (b, 0, h))
    out = pl.pallas_call(
        functools.partial(_attn_prompt_kernel, seq=seq, n_tiles=n_tiles, lam_init=lam_init),
        grid=(N_HEADS_B, batch),
        in_specs=[pl.BlockSpec((None, 4, DK), lambda h, b: (layer, 0, 0)),
                  head, head, head, head,
                  pl.BlockSpec((None, seq + N_META, Q_TILE), lambda h, b: (h, 0, 0)),
                  pl.BlockSpec((None, DV, 1), lambda h, b: (layer, 0, 0))],
        out_specs=head,
        out_shape=jax.ShapeDtypeStruct((batch, seq, width), BF16),
        scratch_shapes=[pltpu.VMEM((2 * DK, tp), BF16),
                        pltpu.VMEM((2 * DK, tp), BF16),
                        pltpu.VMEM((DV, tp), BF16),
                        pltpu.VMEM((n_tiles + 1, 2, DV, Q_TILE), F32),
                        pltpu.VMEM((2, 2, N_META + Q_TILE, Q_TILE), F32)],
        compiler_params=pltpu.CompilerParams(
            dimension_semantics=("arbitrary", "arbitrary"), vmem_limit_bytes=VMEM_LIMIT_BYTES),
        name="diff_attention_prompt",
    )(lam_vec, to3(q), to3(kb), to3(vb), to3(gb), slab, subln_col)
    return out.reshape(m, width)


def _attn_sample_kernel(lv_ref, q_ref, kc_ref, vc_ref, kn_ref, vn_ref, gb_ref, bs_ref, g_ref, o_ref,
                        kb_s, vb_s, s_s, *, past, dec_seq, lam_init):
    lam = _diff_lambda(lv_ref[...], lam_init)
    width = past + LANES
    lanes = lambda h: slice(h * DV, (h + 1) * DV)
    lane = lax.broadcasted_iota(jnp.int32, (dec_seq, DV), 1)
    zero = jnp.zeros((dec_seq, DV), BF16)
    tail = jnp.zeros((LANES - dec_seq, DV), BF16)

    def scores(h, slot):
        for dst, cache, new in ((kb_s, kc_ref, kn_ref), (vb_s, vc_ref, vn_ref)):
            dst[slot, 0:past, :] = cache[pl.ds(h, past, stride=N_HEADS_B), :].astype(BF16)
            dst[slot, past:past + dec_seq, :] = new[:, lanes(h)]
            dst[slot, past + dec_seq:width, :] = tail
        qv = q_ref[:, lanes(h)]
        for m, qm in enumerate((jnp.where(lane < DK, qv, zero), jnp.where(lane >= DK, qv, zero))):
            s = lax.dot_general(qm, kb_s[slot], (((1,), (1,)), ((), ())), preferred_element_type=F32)
            s_s[slot, m] = s + bs_ref[h]

    def fold(h, slot):
        probs = []
        for m in range(2):
            s = s_s[slot, m]
            mx = jnp.max(s, axis=-1, keepdims=True)
            p = jnp.exp2(s - mx)
            probs.append(p * (1.0 / jnp.sum(p, axis=-1, keepdims=True)))
        pd = (probs[0] - lam * probs[1]).astype(BF16)
        o = jnp.dot(pd, vb_s[slot], preferred_element_type=F32)
        ms = jnp.mean(o * o, axis=-1, keepdims=True)
        on = ((o * lax.rsqrt(ms + EPS)) * g_ref[...]) * (1.0 - lam_init)
        gb = gb_ref[:, lanes(h)].astype(F32)
        o_ref[:, lanes(h)] = (on * (gb * _sigmoid(gb))).astype(o_ref.dtype)

    scores(0, 0)
    for h in range(N_HEADS_B):
        if h + 1 < N_HEADS_B:
            scores(h + 1, (h + 1) % 2)
        fold(h, h % 2)


def _attn_sample(q, kb, vb, gb, cache_k, cache_v, slab, lam_vec, subln_row, layer, lam_init,
                 batch, dec_seq):
    m, width = q.shape
    depth, _, past = cache_k.shape[:3]
    ck = cache_k.reshape(depth, batch, past * N_HEADS_B, DV)
    cv = cache_v.reshape(depth, batch, past * N_HEADS_B, DV)
    rows = pl.BlockSpec((dec_seq, width), lambda b: (b, 0))
    cache = pl.BlockSpec((None, None, past * N_HEADS_B, DV), lambda b: (layer, b, 0, 0))
    return pl.pallas_call(
        functools.partial(_attn_sample_kernel, past=past, dec_seq=dec_seq, lam_init=lam_init),
        grid=(batch,),
        in_specs=[pl.BlockSpec((None, 4, DK), lambda b: (layer, 0, 0)),
                  rows, cache, cache, rows, rows, rows,
                  pl.BlockSpec((N_HEADS_B, dec_seq, past + LANES), lambda b: (0, 0, 0)),
                  pl.BlockSpec((None, 1, DV), lambda b: (layer, 0, 0))],
        out_specs=rows,
        out_shape=jax.ShapeDtypeStruct((m, width), BF16),
        scratch_shapes=[pltpu.VMEM((2, past + LANES, DV), BF16),
                        pltpu.VMEM((2, past + LANES, DV), BF16),
                        pltpu.VMEM((2, 2, dec_seq, past + LANES), F32)],
        compiler_params=pltpu.CompilerParams(
            dimension_semantics=("arbitrary",), vmem_limit_bytes=VMEM_LIMIT_BYTES),
        name="diff_attention_sample",
    )(lam_vec, q, ck, cv, kb, vb, gb, slab, subln_row)


def _outproj_kernel(ya_ref, yb_ref, x_ref, w_ref, g_ref, o_ref, y_s, *, col_chunk):
    d_a = ya_ref.shape[-1]
    d = o_ref.shape[-1]
    for c in range(0, d, col_chunk):
        y_s[:, c:c + col_chunk] = (
            jnp.dot(ya_ref[...], w_ref[0:d_a, c:c + col_chunk], preferred_element_type=F32)
            + jnp.dot(yb_ref[...], w_ref[d_a:, c:c + col_chunk], preferred_element_type=F32))
    y = y_s[...]
    ms = jnp.mean(y * y, axis=-1, keepdims=True)
    o_ref[...] = x_ref[...] + (y * lax.rsqrt(ms + EPS)) * g_ref[...]


def _outproj(ya, yb, x2d, w_out_b, post_g, layer):
    m, d = x2d.shape
    tm = _row_tile(m, 384)
    half = pl.BlockSpec((tm, d // 2), lambda i: (i, 0))
    full = pl.BlockSpec((tm, d), lambda i: (i, 0))
    return pl.pallas_call(
        functools.partial(_outproj_kernel, col_chunk=512),
        grid=(m // tm,),
        in_specs=[half, half, full,
                  pl.BlockSpec((None, d, d), lambda i: (layer, 0, 0)),
                  pl.BlockSpec((None, 1, d), lambda i: (layer, 0, 0))],
        out_specs=full,
        out_shape=jax.ShapeDtypeStruct((m, d), F32),
        scratch_shapes=[pltpu.VMEM((tm, d), F32)],
        compiler_params=pltpu.CompilerParams(
            dimension_semantics=("arbitrary",), vmem_limit_bytes=VMEM_LIMIT_BYTES),
        name="out_projection",
    )(ya, yb, x2d, w_out_b, post_g)


def _outproj_main_kernel(ya_ref, yb_ref, x_ref, w_ref, g_ref, o_ref, y_s, *, col_chunk):
    _outproj_kernel(ya_ref.at[0], yb_ref.at[0], x_ref.at[0], w_ref, g_ref, o_ref, y_s,
                    col_chunk=col_chunk)


def _outproj_main(ya, yb, x2d, w_out_b, post_g, layer, batch, seq):
    m, d = x2d.shape
    main = seq - N_META
    tm = _row_tile(main, 512)
    assert tm % N_META == 0
    row0 = lambda j: (j * (tm // N_META) + 1) * N_META
    win = lambda w: pl.BlockSpec((pl.Element(1), pl.Element(tm), pl.Element(w)),
                                 lambda b, j: (b, row0(j), 0))
    return pl.pallas_call(
        functools.partial(_outproj_main_kernel, col_chunk=512),
        grid=(batch, main // tm),
        in_specs=[win(d // 2), win(d // 2), win(d),
                  pl.BlockSpec((None, d, d), lambda b, j: (layer, 0, 0)),
                  pl.BlockSpec((None, 1, d), lambda b, j: (layer, 0, 0))],
        out_specs=pl.BlockSpec((None, tm, d), lambda b, j: (b, j, 0)),
        out_shape=jax.ShapeDtypeStruct((batch, main, d), F32),
        scratch_shapes=[pltpu.VMEM((tm, d), F32)],
        compiler_params=pltpu.CompilerParams(
            dimension_semantics=("arbitrary", "arbitrary"), vmem_limit_bytes=VMEM_LIMIT_BYTES),
        name="out_projection_main",
    )(ya.reshape(batch, seq, d // 2), yb.reshape(batch, seq, d // 2), x2d.reshape(batch, seq, d),
      w_out_b, post_g)


def _block_diag_gates(w):
    depth, nb, bs, _ = w.shape
    per = GATE_BLOCK // bs
    w5 = w.reshape(depth, nb // per, per, bs, bs)
    eye = jnp.eye(per, dtype=w.dtype)
    bd = jnp.einsum('lgaij,ab->lgaibj', w5, eye)
    return bd.reshape(depth, nb // per, GATE_BLOCK, GATE_BLOCK).astype(BF16)


def kernel(x_prompt, x_sample, cache_k, cache_v, state_conv, state_rglru, meta, rel_bias, pre_g, post_g,
           w_in, conv_w, conv_b, gate_r_w, gate_r_b, gate_i_w, gate_i_b, rglru_lam, lam_q1, lam_k1,
           lam_q2, lam_k2, subln_g, w_out):
    batch, seq0, d = x_prompt.shape
    dec_batch, dec_seq, _ = x_sample.shape
    depth = w_in.shape[0]
    past = cache_k.shape[2]
    d_a = d // 2
    seq = seq0 + N_META
    n_tiles = seq0 // Q_TILE

    hp = jnp.concatenate(
        [jnp.broadcast_to(meta.astype(x_prompt.dtype)[None], (batch, N_META, d)), x_prompt],
        axis=1).reshape(batch * seq, d)
    hs = x_sample.reshape(dec_batch * dec_seq, d)

    w_in_b = w_in.astype(BF16)
    w_out_b = w_out.astype(BF16)
    wr_bd = _block_diag_gates(gate_r_w)
    wi_bd = _block_diag_gates(gate_i_w)
    pre_g3 = pre_g.reshape(depth, 1, d)
    post_g3 = post_g.reshape(depth, 1, d)
    conv_b3 = conv_b.reshape(depth, 1, d_a)
    br3 = gate_r_b.reshape(depth, 1, d_a)
    bi3 = gate_i_b.reshape(depth, 1, d_a)
    lam3 = rglru_lam.reshape(depth, 1, d_a)
    lam_vec = jnp.stack([lam_q1, lam_k1, lam_q2, lam_k2], axis=1)
    subln_col = subln_g.reshape(depth, DV, 1)
    subln_row = subln_g.reshape(depth, 1, DV)

    slab_p, slab_s = _bias_slabs(rel_bias, n_tiles, past, dec_seq)

    zero_buf = jnp.zeros((batch, SUBLANES, d_a), F32)
    zero_h = jnp.zeros((batch, 1, d_a), F32)
    pad_rows = SUBLANES - (CONV_W - 1)
    state_conv8 = jnp.pad(state_conv, ((0, 0), (0, 0), (pad_rows, 0), (0, 0)))

    kv_p = None
    kv_s = None
    cp_l, rp_l, cs_l, rs_l = [], [], [], []
    for l in range(depth):
        lam_init = 0.8 - 0.6 * math.exp(-0.3 * l)
        xa, ga, q, kb, vb, gb, k5, v5 = _inproj(hp, pre_g3, w_in_b, l, kv_p, depth)
        kv_p = (k5, v5)
        ya, cp, rp = _rglru(xa, ga, zero_buf, zero_h, conv_w, conv_b3, wr_bd, br3, wi_bd, bi3, lam3,
                            l, batch, seq)
        yb = _attn_prompt(q, kb, vb, gb, slab_p, lam_vec, subln_col, l, lam_init, batch, seq)
        if l + 1 < depth:
            hp = _outproj(ya, yb, hp, w_out_b, post_g3, l)
        else:
            y_prompt = _outproj_main(ya, yb, hp, w_out_b, post_g3, l, batch, seq)
        cp_l.append(cp)
        rp_l.append(rp)
        xa, ga, q, kb, vb, gb, k5, v5 = _inproj(hs, pre_g3, w_in_b, l, kv_s, depth)
        kv_s = (k5, v5)
        ya, cs, rs = _rglru(xa, ga, state_conv8[l], state_rglru[l][:, None, :], conv_w, conv_b3,
                            wr_bd, br3, wi_bd, bi3, lam3, l, dec_batch, dec_seq)
        yb = _attn_sample(q, kb, vb, gb, cache_k, cache_v, slab_s, lam_vec, subln_row, l, lam_init,
                          dec_batch, dec_seq)
        hs = _outproj(ya, yb, hs, w_out_b, post_g3, l)
        cs_l.append(cs)
        rs_l.append(rs)

    y_sample = hs.reshape(dec_batch, dec_seq, d)
    k_prompt = kv_p[0].reshape(depth, batch, seq, N_HEADS_B, 2 * DK)
    v_prompt = kv_p[1].reshape(depth, batch, seq, N_HEADS_B, DV)
    k_sample = kv_s[0].reshape(depth, dec_batch, dec_seq, N_HEADS_B, 2 * DK)
    v_sample = kv_s[1].reshape(depth, dec_batch, dec_seq, N_HEADS_B, DV)
    return (y_prompt, y_sample, k_prompt, v_prompt, jnp.stack(cp_l), jnp.stack(rp_l),
            k_sample, v_sample, jnp.stack(cs_l), jnp.stack(rs_l))
```

```python
import functools
import math

import jax
import jax.numpy as jnp
from jax import lax
from jax.experimental import pallas as pl
from jax.experimental.pallas import tpu as pltpu

F32 = jnp.float32
BF16 = jnp.bfloat16

CHUNK = 64
N_META = 16
N_BLOCKS_A = 16
CONV_W = 4
RG_C = 8.0
N_HEADS_B = 8
DK = 64
DV = 2 * DK
NUM_BUCKETS = 32
REL_MAX_DIST = 1024
EPS = 1e-6

LANES = 128
SUBLANES = 8
BF16_ROWS = 16
VMEM_LIMIT_BYTES = 56 * 1024 * 1024

LOG2E = math.log2(math.e)
Q_TILE = 256
FAR_REL = math.ceil((NUM_BUCKETS // 4) * (REL_MAX_DIST / (NUM_BUCKETS // 4))
                    ** ((NUM_BUCKETS // 4 - 1) / (NUM_BUCKETS // 4))) + CHUNK
GATE_BLOCK = 256
RGLRU_SCRATCH_BYTES = 16 * 1024 * 1024


def _sigmoid(x):
    return 1.0 / (1.0 + jnp.exp(-x))


def _row_tile(m, cap=512):
    best = None
    for t in range(BF16_ROWS, min(m, cap) + 1, BF16_ROWS):
        if m % t == 0:
            best = t
    assert best is not None, m
    return best


def _bias_lookup(rel, tab_row):
    half = NUM_BUCKETS // 2
    max_exact = half // 2
    ret = jnp.where(rel > 0, half, 0).astype(jnp.int32)
    n = jnp.abs(rel)
    nf = jnp.maximum(n, 1).astype(F32)
    large = max_exact + (jnp.log(nf / max_exact) / math.log(REL_MAX_DIST / max_exact)
                         * (half - max_exact)).astype(jnp.int32)
    large = jnp.minimum(large, half - 1)
    bucket = ret + jnp.where(n < max_exact, n, large)
    tab = jnp.broadcast_to(tab_row, (rel.shape[0], LANES))
    return jnp.concatenate(
        [jnp.take_along_axis(tab, bucket[:, c:c + LANES], axis=1) for c in range(0, rel.shape[1], LANES)],
        axis=1)


def _prompt_slab_kernel(tab_ref, out_ref, *, n_tiles):
    seq = Q_TILE * n_tiles + N_META
    diag0 = Q_TILE * (n_tiles - 1) + N_META
    far_bucket = NUM_BUCKETS // 2 - 1
    tab_row = tab_ref[...] - tab_ref[:, far_bucket:far_bucket + 1]
    r0 = 0
    while r0 < seq:
        blk = min(LANES, seq - r0)
        r = lax.broadcasted_iota(jnp.int32, (blk, Q_TILE), 0) + r0
        c = lax.broadcasted_iota(jnp.int32, (blk, Q_TILE), 1)
        val = _bias_lookup(r - c - diag0, tab_row) * LOG2E
        rp = r - diag0
        masked = jnp.logical_and(rp >= 0, (rp // CHUNK) > (c // CHUNK))
        out_ref[r0:r0 + blk, :] = jnp.where(masked, -jnp.inf, val)
        r0 += blk
    k = lax.broadcasted_iota(jnp.int32, (N_META, Q_TILE), 0)
    c = lax.broadcasted_iota(jnp.int32, (N_META, Q_TILE), 1)
    out_ref[seq:seq + N_META, :] = _bias_lookup(k - c, tab_row) * LOG2E


def _sample_slab_kernel(tab_ref, out_ref, *, past, dec_seq):
    width = out_ref.shape[-1]
    for c0 in range(0, width, 512):
        w = min(512, width - c0)
        t = lax.broadcasted_iota(jnp.int32, (dec_seq, w), 0)
        k = lax.broadcasted_iota(jnp.int32, (dec_seq, w), 1) + c0
        val = _bias_lookup(k - past - t, tab_ref[...]) * LOG2E
        out_ref[:, c0:c0 + w] = jnp.where(k >= past + dec_seq, -jnp.inf, val)


def _bias_slabs(rel_bias, n_tiles, past, dec_seq):
    rows = Q_TILE * n_tiles + 2 * N_META
    tab = jnp.pad(rel_bias.T, ((0, 0), (0, LANES - NUM_BUCKETS)))[:, None, :]
    tab_spec = pl.BlockSpec((None, 1, LANES), lambda h: (h, 0, 0))
    prompt = pl.pallas_call(
        functools.partial(_prompt_slab_kernel, n_tiles=n_tiles),
        grid=(N_HEADS_B,),
        in_specs=[tab_spec],
        out_specs=pl.BlockSpec((None, rows, Q_TILE), lambda h: (h, 0, 0)),
        out_shape=jax.ShapeDtypeStruct((N_HEADS_B, rows, Q_TILE), F32),
        name="prompt_bias_slab",
    )(tab)
    width = past + LANES
    sample = pl.pallas_call(
        functools.partial(_sample_slab_kernel, past=past, dec_seq=dec_seq),
        grid=(N_HEADS_B,),
        in_specs=[tab_spec],
        out_specs=pl.BlockSpec((None, dec_seq, width), lambda h: (h, 0, 0)),
        out_shape=jax.ShapeDtypeStruct((N_HEADS_B, dec_seq, width), F32),
        name="sample_bias_slab",
    )(tab)
    return prompt, sample


def _inproj_kernel(x_ref, g_ref, w_ref, *rest, col_chunk):
    xa_ref, ga_ref, q_ref, kb_ref, vb_ref, gb_ref, k5_ref, v5_ref, u_s = rest[-9:]
    tm = x_ref.shape[0]

    def normalise(lo, n):
        x = x_ref[lo:lo + n, :]
        ms = jnp.mean(x * x, axis=-1, keepdims=True)
        u_s[lo:lo + n, :] = ((x * lax.rsqrt(ms + EPS)) * g_ref[...]).astype(BF16)

    chunks = []
    col = 0
    for ref, scale, ref5 in ((xa_ref, None, None), (ga_ref, None, None),
                             (q_ref, (DK ** -0.5) * LOG2E, None),
                             (kb_ref, None, k5_ref), (vb_ref, None, v5_ref), (gb_ref, None, None)):
        for c in range(0, ref.shape[-1], col_chunk):
            chunks.append((ref, scale, ref5, col + c, c))
        col += ref.shape[-1]

    def project(lo, n, chunk):
        ref, scale, ref5, wcol, c = chunk
        acc = jnp.dot(u_s[lo:lo + n, :], w_ref[:, wcol:wcol + col_chunk], preferred_element_type=F32)
        if scale is not None:
            acc = acc * scale
        ref[lo:lo + n, c:c + col_chunk] = acc.astype(ref.dtype)
        if ref5 is not None:
            for j in range(col_chunk // DV):
                head = c // DV + j
                ref5[pl.ds(lo * N_HEADS_B + head, n, stride=N_HEADS_B), :] = acc[:, j * DV:(j + 1) * DV]

    normalise(0, tm)
    for chunk in chunks:
        project(0, tm, chunk)


def _inproj(x2d, pre_g, w_in_b, w_layer, layer, kv_stacks, depth):
    m, d = x2d.shape
    d_a = d // 2
    width = N_HEADS_B * DV
    in_cols = w_in_b.shape[-1]
    tm = _row_tile(m, 384)
    row = lambda w: pl.BlockSpec((tm, w), lambda i: (i, 0))
    stack = pl.BlockSpec((None, tm * N_HEADS_B, DV), lambda i: (layer, i, 0))
    in_specs = [
        row(d),
        pl.BlockSpec((None, 1, d), lambda i: (layer, 0, 0)),
        pl.BlockSpec((None, d, in_cols), lambda i: (w_layer, 0, 0), pipeline_mode=pl.Buffered(1)),
    ]
    args = [x2d, pre_g, w_in_b]
    aliases = {}
    if kv_stacks is not None:
        in_specs += [pl.BlockSpec(memory_space=pl.ANY)] * 2
        args += list(kv_stacks)
        aliases = {3: 6, 4: 7}
    out_shape = [
        jax.ShapeDtypeStruct((m, d_a), F32),
        jax.ShapeDtypeStruct((m, d_a), F32),
        jax.ShapeDtypeStruct((m, width), BF16),
        jax.ShapeDtypeStruct((m, width), BF16),
        jax.ShapeDtypeStruct((m, width), BF16),
        jax.ShapeDtypeStruct((m, width), BF16),
        jax.ShapeDtypeStruct((depth, m * N_HEADS_B, DV), F32),
        jax.ShapeDtypeStruct((depth, m * N_HEADS_B, DV), F32),
    ]
    out_specs = [row(d_a), row(d_a), row(width), row(width), row(width), row(width), stack, stack]
    return pl.pallas_call(
        functools.partial(_inproj_kernel, col_chunk=512),
        grid=(m // tm,),
        in_specs=in_specs,
        out_specs=out_specs,
        out_shape=out_shape,
        scratch_shapes=[pltpu.VMEM((tm, d), BF16)],
        input_output_aliases=aliases,
        compiler_params=pltpu.CompilerParams(
            dimension_semantics=("arbitrary",), vmem_limit_bytes=VMEM_LIMIT_BYTES),
        name="in_projection",
    )(*args)


def _rglru_kernel(xa_ref, ga_ref, buf_ref, h0_ref, cw_ref, cb_ref, wr_ref, br_ref, wi_ref, bi_ref,
                  lam_ref, *rest, seq, seg, g_block, n_cast):
    xp_s, a_s, b_s, hs_s, as_s, hn_s = rest[-6:]
    outs = rest[-6 - 3 - n_cast:-6]
    y_ref, nbuf_ref, hl_ref = outs[:3]
    for src_ref, dst_ref in zip(rest[:n_cast], outs[3:]):
        dst_ref[...] = src_ref[...].astype(dst_ref.dtype)
    c = xa_ref.shape[-1]
    nl = c // LANES
    lanes = lambda j: slice(j * LANES, (j + 1) * LANES)
    pad = SUBLANES * seg - seq
    for j in range(nl):
        xp_s[j, 0:SUBLANES, :] = buf_ref[:, lanes(j)]
        xp_s[j, SUBLANES:SUBLANES + seq, :] = xa_ref[:, lanes(j)]
        if pad:
            xp_s[j, SUBLANES + seq:, :] = jnp.zeros((pad, LANES), F32)
        nbuf_ref[:, lanes(j)] = xp_s[j, seq:seq + SUBLANES, :]

    neg = -lam_ref[...]
    softplus = jnp.maximum(neg, 0.0) + jnp.log1p(jnp.exp(-jnp.abs(neg)))
    cneg = -RG_C * softplus
    cb = cb_ref[...]
    taps = [cw_ref[j:j + 1, :] for j in range(CONV_W)]

    cache = {}

    def rows_at(g):
        if g not in cache:
            cache[g] = jnp.concatenate(
                [xp_s[j, pl.ds(SUBLANES + g, SUBLANES, stride=seg), :] for j in range(nl)], axis=1)
        return cache[g]

    for g0 in range(0, seg, g_block):
        pieces = []
        for g in range(g0, g0 + g_block):
            xc = cb
            for j in range(CONV_W):
                xc = xc + rows_at(g - (CONV_W - 1) + j) * taps[j]
            pieces.append(xc)
        xc = jnp.concatenate(pieces, axis=0)
        xcb = xc.astype(BF16)

        def gate(w_ref, b_ref):
            z = [jnp.dot(xcb[:, k * GATE_BLOCK:(k + 1) * GATE_BLOCK], w_ref[k],
                         preferred_element_type=F32) for k in range(c // GATE_BLOCK)]
            return _sigmoid(jnp.concatenate(z, axis=1) + b_ref[...])

        r = gate(wr_ref, br_ref)
        ig = gate(wi_ref, bi_ref)
        log_a = r * cneg
        a = jnp.exp(log_a)
        lo, hi = g0 * SUBLANES, (g0 + g_block) * SUBLANES
        a_s[lo:hi, :] = a
        e = -jnp.tanh(log_a) * (a * a + 1.0)
        root = jnp.where(e > 0.0, e * lax.rsqrt(e), 0.0)
        b_s[lo:hi, :] = root * (ig * xc)

    rowid = lax.broadcasted_iota(jnp.int32, (SUBLANES, c), 0)
    first = rowid == 0

    def scan_body(g, carry):
        h, acc = carry
        r = pl.multiple_of(g * SUBLANES, SUBLANES)
        a = a_s[pl.ds(r, SUBLANES), :]
        h = a * h + b_s[pl.ds(r, SUBLANES), :]
        acc = a * acc
        hs_s[pl.ds(r, SUBLANES), :] = h
        as_s[pl.ds(r, SUBLANES), :] = acc
        return h, acc

    h_init = jnp.where(first, jnp.broadcast_to(h0_ref[...], (SUBLANES, c)), 0.0)
    end_h, end_a = lax.fori_loop(0, seg, scan_body, (h_init, jnp.ones((SUBLANES, c), F32)), unroll=8)

    d = jnp.zeros((SUBLANES, c), F32)
    for _ in range(SUBLANES - 1):
        d = jnp.where(first, 0.0, pltpu.roll(end_h + end_a * d, 1, 0))

    def fix_body(g, carry):
        r = pl.multiple_of(g * SUBLANES, SUBLANES)
        h = hs_s[pl.ds(r, SUBLANES), :] + as_s[pl.ds(r, SUBLANES), :] * d
        for j in range(nl):
            hn_s[j, pl.ds(g, SUBLANES, stride=seg), :] = h[:, lanes(j)]
        return carry

    lax.fori_loop(0, seg, fix_body, 0, unroll=8)

    for j in range(nl):
        ga = ga_ref[:, lanes(j)]
        y_ref[:, lanes(j)] = (hn_s[j, 0:seq, :] * (ga * _sigmoid(ga))).astype(y_ref.dtype)
        hl_ref[:, lanes(j)] = hn_s[j, seq - 1:seq, :]


def _rglru(xa, ga, buf8, h0, conv_w, conv_b, wr_bd, br, wi_bd, bi, lam, layer, batch, seq, casts=()):
    m, d_a = xa.shape
    xa3 = xa.reshape(batch, seq, d_a)
    ga3 = ga.reshape(batch, seq, d_a)
    seg = pl.cdiv(pl.cdiv(seq, SUBLANES), SUBLANES) * SUBLANES
    g_block = max(t for t in range(1, min(seg, 44) + 1) if seg % t == 0)
    cw = max(w for w in range(GATE_BLOCK, d_a + 1, GATE_BLOCK)
             if d_a % w == 0 and (w == GATE_BLOCK or 6 * 4 * SUBLANES * seg * w <= RGLRU_SCRATCH_BYTES))
    nc = d_a // cw
    nl = cw // LANES
    chan = lambda rows: pl.BlockSpec((None, rows, cw), lambda b, c: (b, 0, c))
    par = lambda rows: pl.BlockSpec((None, rows, cw), lambda b, c: (layer, 0, c))
    wspec = pl.BlockSpec((None, cw // GATE_BLOCK, GATE_BLOCK, GATE_BLOCK), lambda b, c: (layer, c, 0, 0))
    in_specs = [chan(seq), chan(seq), chan(SUBLANES), chan(1), par(CONV_W), par(1),
                wspec, par(1), wspec, par(1), par(1)]
    args = [xa3, ga3, buf8, h0, conv_w, conv_b, wr_bd, br, wi_bd, bi, lam]
    out_specs = [chan(seq), chan(SUBLANES), chan(1)]
    out_shape = [jax.ShapeDtypeStruct((batch, seq, d_a), BF16),
                 jax.ShapeDtypeStruct((batch, SUBLANES, d_a), F32),
                 jax.ShapeDtypeStruct((batch, 1, d_a), F32)]
    steps = batch * nc
    aliases = {}
    for w, src_layer, _ in casts:
        rows = w.shape[1] // steps
        assert rows * steps == w.shape[1] and rows % BF16_ROWS == 0, (w.shape, steps)
        slab = pl.BlockSpec((None, rows, w.shape[2]), lambda b, c, sl=src_layer: (sl, b * nc + c, 0))
        in_specs.append(slab)
        args.append(w)
        out_specs.append(slab)
        out_shape.append(jax.ShapeDtypeStruct(w.shape, BF16))
    for k, (_, _, stack) in enumerate(casts):
        if stack is not None:
            aliases[len(args)] = 3 + k
            in_specs.append(pl.BlockSpec(memory_space=pl.ANY))
            args.append(stack)
    y, nbuf, hl, *filled = pl.pallas_call(
        functools.partial(_rglru_kernel, seq=seq, seg=seg, g_block=g_block, n_cast=len(casts)),
        grid=(batch, nc),
        in_specs=in_specs,
        out_specs=out_specs,
        out_shape=out_shape,
        input_output_aliases=aliases,
        scratch_shapes=[pltpu.VMEM((nl, SUBLANES * (seg + 1), LANES), F32),
                        pltpu.VMEM((SUBLANES * seg, cw), F32),
                        pltpu.VMEM((SUBLANES * seg, cw), F32),
                        pltpu.VMEM((SUBLANES * seg, cw), F32),
                        pltpu.VMEM((SUBLANES * seg, cw), F32),
                        pltpu.VMEM((nl, SUBLANES * seg, LANES), F32)],
        compiler_params=pltpu.CompilerParams(
            dimension_semantics=("arbitrary", "arbitrary"), vmem_limit_bytes=VMEM_LIMIT_BYTES),
        name="rglru_mixer",
    )(*args)
    return (y.reshape(m, d_a), nbuf[:, SUBLANES - (CONV_W - 1):, :], hl[:, 0, :], *filled)


def _diff_lambda(lv, lam_init):
    s1 = jnp.sum(lv[0:1, :] * lv[1:2, :], axis=-1, keepdims=True)
    s2 = jnp.sum(lv[2:3, :] * lv[3:4, :], axis=-1, keepdims=True)
    return jnp.exp(s1) - jnp.exp(s2) + lam_init


def _attn_prompt_kernel(lv_ref, q_ref, k_ref, v_ref, gb_ref, bt_ref, g_ref, o_ref,
                        q1t_s, q2t_s, vt_s, acc_s, s_s, *, seq, n_tiles, lam_init):
    gap = LANES - N_META
    lam = _diff_lambda(lv_ref[...], lam_init)

    row = lax.broadcasted_iota(jnp.int32, (2 * DK, LANES), 0)
    zero = jnp.zeros((2 * DK, LANES), BF16)
    pad = jnp.zeros((gap, DV), BF16)
    for j in range(2 * n_tiles + 1):
        cols = slice(LANES * j, LANES * (j + 1))
        if j == 0:
            qb = jnp.concatenate([q_ref[0:N_META, :], pad], axis=0)
            vb = jnp.concatenate([v_ref[0:N_META, :], pad], axis=0)
        else:
            rows = slice(N_META + LANES * (j - 1), N_META + LANES * j)
            qb = q_ref[rows, :]
            vb = v_ref[rows, :]
        qt = qb.T
        q1t_s[:, cols] = jnp.where(row < DK, qt, zero)
        q2t_s[:, cols] = jnp.where(row >= DK, qt, zero)
        vt_s[:, cols] = vb.T

    gcol = g_ref[...] * (1.0 - lam_init)
    meta_pad = jnp.zeros((gap, Q_TILE), BF16)

    tiles = [(0, LANES, 0, seq, 0, N_META)]
    for i in range(n_tiles):
        tiles.append((LANES + Q_TILE * i, Q_TILE, i + 1, Q_TILE * (n_tiles - 1 - i),
                      N_META + Q_TILE * i, Q_TILE))
    q_maps = (q1t_s, q2t_s)
    state = {}

    def key_rows(t, j):
        if j == 0:
            return 0, N_META + (Q_TILE if t else 0)
        return N_META + Q_TILE * j, Q_TILE

    def scores(t, j, slot):
        q_col, n_q, _, b_lo, _, _ = tiles[t]
        r0, nr = key_rows(t, j)
        far = t > 0 and r0 + nr - 1 - tiles[t][4] <= -FAR_REL
        cms = []
        for m in range(2):
            s = jnp.dot(k_ref[r0:r0 + nr, :], q_maps[m][:, q_col:q_col + n_q],
                        preferred_element_type=F32)
            if not far:
                s = s + bt_ref[b_lo + r0:b_lo + r0 + nr, 0:n_q]
            s_s[slot, m, 0:nr, 0:n_q] = s
            cms.append(jnp.max(s, axis=0, keepdims=True))
        return cms

    def fold(t, j, slot, cms):
        _, n_q, _, _, _, _ = tiles[t]
        r0, nr = key_rows(t, j)
        ops = []
        for m in range(2):
            if j == 0:
                m_new, alpha = cms[m], None
                s = s_s[slot, m, 0:nr, 0:n_q]
                p = jnp.exp2(s - m_new)
                den = jnp.sum(p, axis=0, keepdims=True)
                pb = p.astype(BF16)
                parts = [pb[0:N_META], meta_pad[:, 0:n_q]]
                if nr > N_META:
                    parts.append(pb[N_META:nr])
                pb = jnp.concatenate(parts, axis=0)
                c0, nc = 0, LANES + nr - N_META
            else:
                mx, den = state[t, m]
                m_new = jnp.maximum(mx, cms[m])
                alpha = jnp.exp2(mx - m_new)
                p = jnp.exp2(s_s[slot, m, 0:nr, 0:n_q] - m_new)
                den = alpha * den + jnp.sum(p, axis=0, keepdims=True)
                pb = p.astype(BF16)
                c0, nc = LANES + Q_TILE * j, Q_TILE
            state[t, m] = (m_new, den)
            ops.append((alpha, pb, c0, nc))
        for m, (alpha, pb, c0, nc) in enumerate(ops):
            o = jnp.dot(vt_s[:, c0:c0 + nc], pb, preferred_element_type=F32)
            if alpha is None:
                acc_s[t, m, :, 0:n_q] = o
            else:
                acc_s[t, m, :, 0:n_q] = alpha * acc_s[t, m, :, 0:n_q] + o

    def finish(t):
        _, n_q, _, _, q_lo, out_rows = tiles[t]
        outs = [acc_s[t, m, :, 0:n_q] * (1.0 / state[t, m][1]) for m in range(2)]
        ot = outs[0] - lam * outs[1]
        ms = jnp.mean(ot * ot, axis=0, keepdims=True)
        on = (ot * lax.rsqrt(ms + EPS)) * gcol
        o_rows = on.T[0:out_rows, :]
        gb = gb_ref[q_lo:q_lo + out_rows, :].astype(F32)
        o_ref[q_lo:q_lo + out_rows, :] = (o_rows * (gb * _sigmoid(gb))).astype(o_ref.dtype)

    jobs = [(0, 0)] + [(t, j) for _, t, j in sorted(
        ((j + 0.5) / tiles[t][2], t, j) for t in range(1, len(tiles)) for j in range(tiles[t][2]))]
    cms_next = scores(*jobs[0], 0)
    for n, (t, j) in enumerate(jobs):
        cms = cms_next
        if n + 1 < len(jobs):
            cms_next = scores(*jobs[n + 1], (n + 1) % 2)
        fold(t, j, n % 2, cms)
        if j == max(tiles[t][2] - 1, 0):
            finish(t)


def _attn_prompt(q, kb, vb, gb, slab, lam_vec, subln_col, layer, lam_init, batch, seq):
    m, width = q.shape
    n_tiles = (seq - N_META) // Q_TILE
    assert N_META + n_tiles * Q_TILE == seq
    tp = (2 * n_tiles + 1) * LANES
    to3 = lambda a: a.reshape(batch, seq, width)
    head = pl.BlockSpec((None, seq, DV), lambda h, b: (b, 0, h))
    out = pl.pallas_call(
        functools.partial(_attn_prompt_kernel, seq=seq, n_tiles=n_tiles, lam_init=lam_init),
        grid=(N_HEADS_B, batch),
        in_specs=[pl.BlockSpec((None, 4, DK), lambda h, b: (layer, 0, 0)),
                  head, head, head, head,
                  pl.BlockSpec((None, seq + N_META, Q_TILE), lambda h, b: (h, 0, 0)),
                  pl.BlockSpec((None, DV, 1), lambda h, b: (layer, 0, 0))],
        out_specs=head,
        out_shape=jax.ShapeDtypeStruct((batch, seq, width), BF16),
        scratch_shapes=[pltpu.VMEM((2 * DK, tp), BF16),
                        pltpu.VMEM((2 * DK, tp), BF16),
                        pltpu.VMEM((DV, tp), BF16),
                        pltpu.VMEM((n_tiles + 1, 2, DV, Q_TILE), F32),
                        pltpu.VMEM((2, 2, N_META + Q_TILE, Q_TILE), F32)],
        compiler_params=pltpu.CompilerParams(
            dimension_semantics=("arbitrary", "arbitrary"), vmem_limit_bytes=VMEM_LIMIT_BYTES),
        name="diff_attention_prompt",
    )(lam_vec, to3(q), to3(kb), to3(vb), to3(gb), slab, subln_col)
    return out.reshape(m, width)


def _attn_sample_kernel(lv_ref, q_ref, kc_ref, vc_ref, kn_ref, vn_ref, gb_ref, bs_ref, g_ref, o_ref,
                        kb_s, vb_s, s_s, *, past, dec_seq, lam_init):
    lam = _diff_lambda(lv_ref[...], lam_init)
    width = past + LANES
    lanes = lambda h: slice(h * DV, (h + 1) * DV)
    lane = lax.broadcasted_iota(jnp.int32, (dec_seq, DV), 1)
    zero = jnp.zeros((dec_seq, DV), BF16)
    tail = jnp.zeros((LANES - dec_seq, DV), BF16)

    def scores(h, slot):
        for dst, cache, new in ((kb_s, kc_ref, kn_ref), (vb_s, vc_ref, vn_ref)):
            dst[slot, 0:past, :] = cache[pl.ds(h, past, stride=N_HEADS_B), :].astype(BF16)
            dst[slot, past:past + dec_seq, :] = new[:, lanes(h)]
            dst[slot, past + dec_seq:width, :] = tail
        qv = q_ref[:, lanes(h)]
        for m, qm in enumerate((jnp.where(lane < DK, qv, zero), jnp.where(lane >= DK, qv, zero))):
            s = lax.dot_general(qm, kb_s[slot], (((1,), (1,)), ((), ())), preferred_element_type=F32)
            s_s[slot, m] = s + bs_ref[h]

    def fold(h, slot):
        probs = []
        for m in range(2):
            s = s_s[slot, m]
            mx = jnp.max(s, axis=-1, keepdims=True)
            p = jnp.exp2(s - mx)
            probs.append(p * (1.0 / jnp.sum(p, axis=-1, keepdims=True)))
        pd = (probs[0] - lam * probs[1]).astype(BF16)
        o = jnp.dot(pd, vb_s[slot], preferred_element_type=F32)
        ms = jnp.mean(o * o, axis=-1, keepdims=True)
        on = ((o * lax.rsqrt(ms + EPS)) * g_ref[...]) * (1.0 - lam_init)
        gb = gb_ref[:, lanes(h)].astype(F32)
        o_ref[:, lanes(h)] = (on * (gb * _sigmoid(gb))).astype(o_ref.dtype)

    scores(0, 0)
    for h in range(N_HEADS_B):
        if h + 1 < N_HEADS_B:
            scores(h + 1, (h + 1) % 2)
        fold(h, h % 2)


def _attn_sample(q, kb, vb, gb, cache_k, cache_v, slab, lam_vec, subln_row, layer, lam_init,
                 batch, dec_seq):
    m, width = q.shape
    depth, _, past = cache_k.shape[:3]
    ck = cache_k.reshape(depth, batch, past * N_HEADS_B, DV)
    cv = cache_v.reshape(depth, batch, past * N_HEADS_B, DV)
    rows = pl.BlockSpec((dec_seq, width), lambda b: (b, 0))
    cache = pl.BlockSpec((None, None, past * N_HEADS_B, DV), lambda b: (layer, b, 0, 0))
    return pl.pallas_call(
        functools.partial(_attn_sample_kernel, past=past, dec_seq=dec_seq, lam_init=lam_init),
        grid=(batch,),
        in_specs=[pl.BlockSpec((None, 4, DK), lambda b: (layer, 0, 0)),
                  rows, cache, cache, rows, rows, rows,
                  pl.BlockSpec((N_HEADS_B, dec_seq, past + LANES), lambda b: (0, 0, 0)),
                  pl.BlockSpec((None, 1, DV), lambda b: (layer, 0, 0))],
        out_specs=rows,
        out_shape=jax.ShapeDtypeStruct((m, width), BF16),
        scratch_shapes=[pltpu.VMEM((2, past + LANES, DV), BF16),
                        pltpu.VMEM((2, past + LANES, DV), BF16),
                        pltpu.VMEM((2, 2, dec_seq, past + LANES), F32)],
        compiler_params=pltpu.CompilerParams(
            dimension_semantics=("arbitrary",), vmem_limit_bytes=VMEM_LIMIT_BYTES),
        name="diff_attention_sample",
    )(lam_vec, q, ck, cv, kb, vb, gb, slab, subln_row)


def _outproj_kernel(ya_ref, yb_ref, x_ref, w_ref, g_ref, o_ref, y_s, *, col_chunk):
    d_a = ya_ref.shape[-1]
    d = o_ref.shape[-1]
    for c in range(0, d, col_chunk):
        y_s[:, c:c + col_chunk] = (
            jnp.dot(ya_ref[...], w_ref[0:d_a, c:c + col_chunk], preferred_element_type=F32)
            + jnp.dot(yb_ref[...], w_ref[d_a:, c:c + col_chunk], preferred_element_type=F32))
    y = y_s[...]
    ms = jnp.mean(y * y, axis=-1, keepdims=True)
    o_ref[...] = x_ref[...] + (y * lax.rsqrt(ms + EPS)) * g_ref[...]


def _outproj(ya, yb, x2d, w_out_b, post_g, layer):
    m, d = x2d.shape
    tm = _row_tile(m, 384)
    half = pl.BlockSpec((tm, d // 2), lambda i: (i, 0))
    full = pl.BlockSpec((tm, d), lambda i: (i, 0))
    return pl.pallas_call(
        functools.partial(_outproj_kernel, col_chunk=512),
        grid=(m // tm,),
        in_specs=[half, half, full,
                  pl.BlockSpec((None, d, d), lambda i: (layer, 0, 0)),
                  pl.BlockSpec((None, 1, d), lambda i: (layer, 0, 0))],
        out_specs=full,
        out_shape=jax.ShapeDtypeStruct((m, d), F32),
        scratch_shapes=[pltpu.VMEM((tm, d), F32)],
        compiler_params=pltpu.CompilerParams(
            dimension_semantics=("arbitrary",), vmem_limit_bytes=VMEM_LIMIT_BYTES),
        name="out_projection",
    )(ya, yb, x2d, w_out_b, post_g)


def _outproj_main_kernel(ya_ref, yb_ref, x_ref, w_ref, g_ref, o_ref, y_s, *, col_chunk):
    _outproj_kernel(ya_ref.at[0], yb_ref.at[0], x_ref.at[0], w_ref, g_ref, o_ref, y_s,
                    col_chunk=col_chunk)


def _outproj_main(ya, yb, x2d, w_out_b, post_g, layer, batch, seq):
    m, d = x2d.shape
    main = seq - N_META
    tm = _row_tile(main, 512)
    assert tm % N_META == 0
    row0 = lambda j: (j * (tm // N_META) + 1) * N_META
    win = lambda w: pl.BlockSpec((pl.Element(1), pl.Element(tm), pl.Element(w)),
                                 lambda b, j: (b, row0(j), 0))
    return pl.pallas_call(
        functools.partial(_outproj_main_kernel, col_chunk=512),
        grid=(batch, main // tm),
        in_specs=[win(d // 2), win(d // 2), win(d),
                  pl.BlockSpec((None, d, d), lambda b, j: (layer, 0, 0)),
                  pl.BlockSpec((None, 1, d), lambda b, j: (layer, 0, 0))],
        out_specs=pl.BlockSpec((None, tm, d), lambda b, j: (b, j, 0)),
        out_shape=jax.ShapeDtypeStruct((batch, main, d), F32),
        scratch_shapes=[pltpu.VMEM((tm, d), F32)],
        compiler_params=pltpu.CompilerParams(
            dimension_semantics=("arbitrary", "arbitrary"), vmem_limit_bytes=VMEM_LIMIT_BYTES),
        name="out_projection_main",
    )(ya.reshape(batch, seq, d // 2), yb.reshape(batch, seq, d // 2), x2d.reshape(batch, seq, d),
      w_out_b, post_g)


def _block_diag_gates(w):
    depth, nb, bs, _ = w.shape
    per = GATE_BLOCK // bs
    w5 = w.reshape(depth, nb // per, per, bs, bs)
    eye = jnp.eye(per, dtype=w.dtype)
    bd = jnp.einsum('lgaij,ab->lgaibj', w5, eye)
    return bd.reshape(depth, nb // per, GATE_BLOCK, GATE_BLOCK).astype(BF16)


def kernel(x_prompt, x_sample, cache_k, cache_v, state_conv, state_rglru, meta, rel_bias, pre_g, post_g,
           w_in, conv_w, conv_b, gate_r_w, gate_r_b, gate_i_w, gate_i_b, rglru_lam, lam_q1, lam_k1,
           lam_q2, lam_k2, subln_g, w_out):
    batch, seq0, d = x_prompt.shape
    dec_batch, dec_seq, _ = x_sample.shape
    depth = w_in.shape[0]
    past = cache_k.shape[2]
    d_a = d // 2
    seq = seq0 + N_META
    n_tiles = seq0 // Q_TILE

    hp = jnp.concatenate(
        [jnp.broadcast_to(meta.astype(x_prompt.dtype)[None], (batch, N_META, d)), x_prompt],
        axis=1).reshape(batch * seq, d)
    hs = x_sample.reshape(dec_batch * dec_seq, d)

    w_in_first = w_in[0:1].astype(BF16)
    w_in_b = None
    w_out_b = None
    wr_bd = _block_diag_gates(gate_r_w)
    wi_bd = _block_diag_gates(gate_i_w)
    pre_g3 = pre_g.reshape(depth, 1, d)
    post_g3 = post_g.reshape(depth, 1, d)
    conv_b3 = conv_b.reshape(depth, 1, d_a)
    br3 = gate_r_b.reshape(depth, 1, d_a)
    bi3 = gate_i_b.reshape(depth, 1, d_a)
    lam3 = rglru_lam.reshape(depth, 1, d_a)
    lam_vec = jnp.stack([lam_q1, lam_k1, lam_q2, lam_k2], axis=1)
    subln_col = subln_g.reshape(depth, DV, 1)
    subln_row = subln_g.reshape(depth, 1, DV)

    slab_p, slab_s = _bias_slabs(rel_bias, n_tiles, past, dec_seq)

    zero_buf = jnp.zeros((batch, SUBLANES, d_a), F32)
    zero_h = jnp.zeros((batch, 1, d_a), F32)
    pad_rows = SUBLANES - (CONV_W - 1)
    state_conv8 = jnp.pad(state_conv, ((0, 0), (0, 0), (pad_rows, 0), (0, 0)))

    kv_p = None
    kv_s = None
    cp_l, rp_l, cs_l, rs_l = [], [], [], []
    for l in range(depth):
        lam_init = 0.8 - 0.6 * math.exp(-0.3 * l)
        w_l = (w_in_first, 0) if l == 0 else (w_in_b, l)
        xa, ga, q, kb, vb, gb, k5, v5 = _inproj(hp, pre_g3, *w_l, l, kv_p, depth)
        kv_p = (k5, v5)
        casts = [(w_out, l, w_out_b)] + ([(w_in, l + 1, w_in_b)] if l + 1 < depth else [])
        ya, cp, rp, *cast = _rglru(xa, ga, zero_buf, zero_h, conv_w, conv_b3, wr_bd, br3, wi_bd, bi3,
                                   lam3, l, batch, seq, casts=casts)
        w_out_b = cast[0]
        if l + 1 < depth:
            w_in_b = cast[1]
        if l:
            w_l = (w_in_b, l)
        yb = _attn_prompt(q, kb, vb, gb, slab_p, lam_vec, subln_col, l, lam_init, batch, seq)
        if l + 1 < depth:
            hp = _outproj(ya, yb, hp, w_out_b, post_g3, l)
        else:
            y_prompt = _outproj_main(ya, yb, hp, w_out_b, post_g3, l, batch, seq)
        cp_l.append(cp)
        rp_l.append(rp)
        xa, ga, q, kb, vb, gb, k5, v5 = _inproj(hs, pre_g3, *w_l, l, kv_s, depth)
        kv_s = (k5, v5)
        ya, cs, rs = _rglru(xa, ga, state_conv8[l], state_rglru[l][:, None, :], conv_w, conv_b3,
                            wr_bd, br3, wi_bd, bi3, lam3, l, dec_batch, dec_seq)
        yb = _attn_sample(q, kb, vb, gb, cache_k, cache_v, slab_s, lam_vec, subln_row, l, lam_init,
                          dec_batch, dec_seq)
        hs = _outproj(ya, yb, hs, w_out_b, post_g3, l)
        cs_l.append(cs)
        rs_l.append(rs)

    y_sample = hs.reshape(dec_batch, dec_seq, d)
    k_prompt = kv_p[0].reshape(depth, batch, seq, N_HEADS_B, 2 * DK)
    v_prompt = kv_p[1].reshape(depth, batch, seq, N_HEADS_B, DV)
    k_sample = kv_s[0].reshape(depth, dec_batch, dec_seq, N_HEADS_B, 2 * DK)
    v_sample = kv_s[1].reshape(depth, dec_batch, dec_seq, N_HEADS_B, DV)
    return (y_prompt, y_sample, k_prompt, v_prompt, jnp.stack(cp_l), jnp.stack(rp_l),
            k_sample, v_sample, jnp.stack(cs_l), jnp.stack(rs_l))
```

```python
import functools
import math

import jax
import jax.numpy as jnp
from jax import lax
from jax.experimental import pallas as pl
from jax.experimental.pallas import tpu as pltpu

F32 = jnp.float32
BF16 = jnp.bfloat16

CHUNK = 64
N_META = 16
N_BLOCKS_A = 16
CONV_W = 4
RG_C = 8.0
N_HEADS_B = 8
DK = 64
DV = 2 * DK
NUM_BUCKETS = 32
REL_MAX_DIST = 1024
EPS = 1e-6

LANES = 128
SUBLANES = 8
BF16_ROWS = 16
VMEM_LIMIT_BYTES = 56 * 1024 * 1024

LOG2E = math.log2(math.e)
Q_TILE = 256
KEY_BLOCK = 2048
PROJ_COLS = 512
FAR_REL = math.ceil((NUM_BUCKETS // 4) * (REL_MAX_DIST / (NUM_BUCKETS // 4))
                    ** ((NUM_BUCKETS // 4 - 1) / (NUM_BUCKETS // 4))) + CHUNK
GATE_BLOCK = 256
RGLRU_SCRATCH_BYTES = 16 * 1024 * 1024


def _sigmoid(x):
    return 1.0 / (1.0 + jnp.exp(-x))


def _row_tile(m, cap=512):
    best = None
    for t in range(BF16_ROWS, min(m, cap) + 1, BF16_ROWS):
        if m % t == 0:
            best = t
    assert best is not None, m
    return best


def _bias_lookup(rel, tab_row):
    half = NUM_BUCKETS // 2
    max_exact = half // 2
    ret = jnp.where(rel > 0, half, 0).astype(jnp.int32)
    n = jnp.abs(rel)
    nf = jnp.maximum(n, 1).astype(F32)
    large = max_exact + (jnp.log(nf / max_exact) / math.log(REL_MAX_DIST / max_exact)
                         * (half - max_exact)).astype(jnp.int32)
    large = jnp.minimum(large, half - 1)
    bucket = ret + jnp.where(n < max_exact, n, large)
    tab = jnp.broadcast_to(tab_row, (rel.shape[0], LANES))
    return jnp.concatenate(
        [jnp.take_along_axis(tab, bucket[:, c:c + LANES], axis=1) for c in range(0, rel.shape[1], LANES)],
        axis=1)


def _prompt_slab_kernel(tab_ref, out_ref, *, n_tiles):
    seq = Q_TILE * n_tiles + N_META
    diag0 = Q_TILE * (n_tiles - 1) + N_META
    far_bucket = NUM_BUCKETS // 2 - 1
    tab_row = tab_ref[...] - tab_ref[:, far_bucket:far_bucket + 1]
    r0 = 0
    while r0 < seq:
        blk = min(LANES, seq - r0)
        r = lax.broadcasted_iota(jnp.int32, (blk, Q_TILE), 0) + r0
        c = lax.broadcasted_iota(jnp.int32, (blk, Q_TILE), 1)
        val = _bias_lookup(r - c - diag0, tab_row) * LOG2E
        rp = r - diag0
        masked = jnp.logical_and(rp >= 0, (rp // CHUNK) > (c // CHUNK))
        out_ref[r0:r0 + blk, :] = jnp.where(masked, -jnp.inf, val)
        r0 += blk
    k = lax.broadcasted_iota(jnp.int32, (N_META, Q_TILE), 0)
    c = lax.broadcasted_iota(jnp.int32, (N_META, Q_TILE), 1)
    out_ref[seq:seq + N_META, :] = _bias_lookup(k - c, tab_row) * LOG2E


def _sample_slab_kernel(tab_ref, out_ref, *, past, dec_seq):
    width = out_ref.shape[-1]
    for c0 in range(0, width, 512):
        w = min(512, width - c0)
        t = lax.broadcasted_iota(jnp.int32, (dec_seq, w), 0)
        k = lax.broadcasted_iota(jnp.int32, (dec_seq, w), 1) + c0
        val = _bias_lookup(k - past - t, tab_ref[...]) * LOG2E
        out_ref[:, c0:c0 + w] = jnp.where(k >= past + dec_seq, -jnp.inf, val)


def _bias_slabs(rel_bias, n_tiles, past, dec_seq):
    rows = Q_TILE * n_tiles + 2 * N_META
    tab = jnp.pad(rel_bias.T, ((0, 0), (0, LANES - NUM_BUCKETS)))[:, None, :]
    tab_spec = pl.BlockSpec((None, 1, LANES), lambda h: (h, 0, 0))
    prompt = pl.pallas_call(
        functools.partial(_prompt_slab_kernel, n_tiles=n_tiles),
        grid=(N_HEADS_B,),
        in_specs=[tab_spec],
        out_specs=pl.BlockSpec((None, rows, Q_TILE), lambda h: (h, 0, 0)),
        out_shape=jax.ShapeDtypeStruct((N_HEADS_B, rows, Q_TILE), F32),
        name="prompt_bias_slab",
    )(tab)
    width = past + LANES
    sample = pl.pallas_call(
        functools.partial(_sample_slab_kernel, past=past, dec_seq=dec_seq),
        grid=(N_HEADS_B,),
        in_specs=[tab_spec],
        out_specs=pl.BlockSpec((None, dec_seq, width), lambda h: (h, 0, 0)),
        out_shape=jax.ShapeDtypeStruct((N_HEADS_B, dec_seq, width), F32),
        name="sample_bias_slab",
    )(tab)
    return prompt, sample


def _inproj_kernel(x_ref, g_ref, w_ref, *rest, col_chunk):
    xa_ref, ga_ref, q_ref, kb_ref, vb_ref, gb_ref, k5_ref, v5_ref, u_s = rest[-9:]
    tm = x_ref.shape[0]

    def normalise(lo, n):
        x = x_ref[lo:lo + n, :]
        ms = jnp.mean(x * x, axis=-1, keepdims=True)
        u_s[lo:lo + n, :] = ((x * lax.rsqrt(ms + EPS)) * g_ref[...]).astype(BF16)

    chunks = []
    col = 0
    for ref, scale, ref5 in ((xa_ref, None, None), (ga_ref, None, None),
                             (q_ref, (DK ** -0.5) * LOG2E, None),
                             (kb_ref, None, k5_ref), (vb_ref, None, v5_ref), (gb_ref, None, None)):
        for c in range(0, ref.shape[-1], col_chunk):
            chunks.append((ref, scale, ref5, col + c, c))
        col += ref.shape[-1]

    def project(lo, n, chunk):
        ref, scale, ref5, wcol, c = chunk
        acc = jnp.dot(u_s[lo:lo + n, :], w_ref[:, wcol:wcol + col_chunk], preferred_element_type=F32)
        if scale is not None:
            acc = acc * scale
        ref[lo:lo + n, c:c + col_chunk] = acc.astype(ref.dtype)
        if ref5 is not None:
            for j in range(col_chunk // DV):
                head = c // DV + j
                ref5[pl.ds(lo * N_HEADS_B + head, n, stride=N_HEADS_B), :] = acc[:, j * DV:(j + 1) * DV]

    normalise(0, tm)
    for chunk in chunks:
        project(0, tm, chunk)


def _inproj(x2d, pre_g, w_in_b, w_layer, layer, kv_stacks, depth):
    m, d = x2d.shape
    d_a = d // 2
    width = N_HEADS_B * DV
    in_cols = w_in_b.shape[-1]
    tm = _row_tile(m, 384)
    row = lambda w: pl.BlockSpec((tm, w), lambda i: (i, 0))
    stack = pl.BlockSpec((None, tm * N_HEADS_B, DV), lambda i: (layer, i, 0))
    in_specs = [
        row(d),
        pl.BlockSpec((None, 1, d), lambda i: (layer, 0, 0)),
        pl.BlockSpec((None, d, in_cols), lambda i: (w_layer, 0, 0), pipeline_mode=pl.Buffered(1)),
    ]
    args = [x2d, pre_g, w_in_b]
    aliases = {}
    if kv_stacks is not None:
        in_specs += [pl.BlockSpec(memory_space=pl.ANY)] * 2
        args += list(kv_stacks)
        aliases = {3: 6, 4: 7}
    out_shape = [
        jax.ShapeDtypeStruct((m, d_a), F32),
        jax.ShapeDtypeStruct((m, d_a), F32),
        jax.ShapeDtypeStruct((m, width), BF16),
        jax.ShapeDtypeStruct((m, width), BF16),
        jax.ShapeDtypeStruct((m, width), BF16),
        jax.ShapeDtypeStruct((m, width), BF16),
        jax.ShapeDtypeStruct((depth, m * N_HEADS_B, DV), F32),
        jax.ShapeDtypeStruct((depth, m * N_HEADS_B, DV), F32),
    ]
    out_specs = [row(d_a), row(d_a), row(width), row(width), row(width), row(width), stack, stack]
    return pl.pallas_call(
        functools.partial(_inproj_kernel, col_chunk=PROJ_COLS),
        grid=(m // tm,),
        in_specs=in_specs,
        out_specs=out_specs,
        out_shape=out_shape,
        scratch_shapes=[pltpu.VMEM((tm, d), BF16)],
        input_output_aliases=aliases,
        compiler_params=pltpu.CompilerParams(
            dimension_semantics=("arbitrary",), vmem_limit_bytes=VMEM_LIMIT_BYTES),
        name="in_projection",
    )(*args)


def _rglru_kernel(xa_ref, ga_ref, buf_ref, h0_ref, cw_ref, cb_ref, wr_ref, br_ref, wi_ref, bi_ref,
                  lam_ref, *rest, seq, seg, g_block, n_cast):
    xp_s, a_s, b_s, hs_s, as_s, hn_s = rest[-6:]
    outs = rest[-6 - 3 - n_cast:-6]
    y_ref, nbuf_ref, hl_ref = outs[:3]
    for src_ref, dst_ref in zip(rest[:n_cast], outs[3:]):
        dst_ref[...] = src_ref[...].astype(dst_ref.dtype)
    c = xa_ref.shape[-1]
    nl = c // LANES
    lanes = lambda j: slice(j * LANES, (j + 1) * LANES)
    pad = SUBLANES * seg - seq
    for j in range(nl):
        xp_s[j, 0:SUBLANES, :] = buf_ref[:, lanes(j)]
        xp_s[j, SUBLANES:SUBLANES + seq, :] = xa_ref[:, lanes(j)]
        if pad:
            xp_s[j, SUBLANES + seq:, :] = jnp.zeros((pad, LANES), F32)
        nbuf_ref[:, lanes(j)] = xp_s[j, seq:seq + SUBLANES, :]

    neg = -lam_ref[...]
    softplus = jnp.maximum(neg, 0.0) + jnp.log1p(jnp.exp(-jnp.abs(neg)))
    cneg = -RG_C * softplus
    cb = cb_ref[...]
    taps = [cw_ref[j:j + 1, :] for j in range(CONV_W)]

    cache = {}

    def rows_at(g):
        if g not in cache:
            cache[g] = jnp.concatenate(
                [xp_s[j, pl.ds(SUBLANES + g, SUBLANES, stride=seg), :] for j in range(nl)], axis=1)
        return cache[g]

    for g0 in range(0, seg, g_block):
        pieces = []
        for g in range(g0, g0 + g_block):
            xc = cb
            for j in range(CONV_W):
                xc = xc + rows_at(g - (CONV_W - 1) + j) * taps[j]
            pieces.append(xc)
        xc = jnp.concatenate(pieces, axis=0)
        xcb = xc.astype(BF16)

        def gate(w_ref, b_ref):
            z = [jnp.dot(xcb[:, k * GATE_BLOCK:(k + 1) * GATE_BLOCK], w_ref[k],
                         preferred_element_type=F32) for k in range(c // GATE_BLOCK)]
            return _sigmoid(jnp.concatenate(z, axis=1) + b_ref[...])

        r = gate(wr_ref, br_ref)
        ig = gate(wi_ref, bi_ref)
        log_a = r * cneg
        a = jnp.exp(log_a)
        lo, hi = g0 * SUBLANES, (g0 + g_block) * SUBLANES
        a_s[lo:hi, :] = a
        e = -jnp.tanh(log_a) * (a * a + 1.0)
        root = jnp.where(e > 0.0, e * lax.rsqrt(e), 0.0)
        b_s[lo:hi, :] = root * (ig * xc)

    rowid = lax.broadcasted_iota(jnp.int32, (SUBLANES, c), 0)
    first = rowid == 0

    def scan_body(g, carry):
        h, acc = carry
        r = pl.multiple_of(g * SUBLANES, SUBLANES)
        a = a_s[pl.ds(r, SUBLANES), :]
        h = a * h + b_s[pl.ds(r, SUBLANES), :]
        acc = a * acc
        hs_s[pl.ds(r, SUBLANES), :] = h
        as_s[pl.ds(r, SUBLANES), :] = acc
        return h, acc

    h_init = jnp.where(first, jnp.broadcast_to(h0_ref[...], (SUBLANES, c)), 0.0)
    end_h, end_a = lax.fori_loop(0, seg, scan_body, (h_init, jnp.ones((SUBLANES, c), F32)), unroll=8)

    d = jnp.zeros((SUBLANES, c), F32)
    for _ in range(SUBLANES - 1):
        d = jnp.where(first, 0.0, pltpu.roll(end_h + end_a * d, 1, 0))

    def fix_body(g, carry):
        r = pl.multiple_of(g * SUBLANES, SUBLANES)
        h = hs_s[pl.ds(r, SUBLANES), :] + as_s[pl.ds(r, SUBLANES), :] * d
        for j in range(nl):
            hn_s[j, pl.ds(g, SUBLANES, stride=seg), :] = h[:, lanes(j)]
        return carry

    lax.fori_loop(0, seg, fix_body, 0, unroll=8)

    for j in range(nl):
        ga = ga_ref[:, lanes(j)]
        y_ref[:, lanes(j)] = (hn_s[j, 0:seq, :] * (ga * _sigmoid(ga))).astype(y_ref.dtype)
        hl_ref[:, lanes(j)] = hn_s[j, seq - 1:seq, :]


def _rglru(xa, ga, buf8, h0, conv_w, conv_b, wr_bd, br, wi_bd, bi, lam, layer, batch, seq, casts=()):
    m, d_a = xa.shape
    xa3 = xa.reshape(batch, seq, d_a)
    ga3 = ga.reshape(batch, seq, d_a)
    seg = pl.cdiv(pl.cdiv(seq, SUBLANES), SUBLANES) * SUBLANES
    g_block = max(t for t in range(1, min(seg, 44) + 1) if seg % t == 0)
    cw = max(w for w in range(GATE_BLOCK, d_a + 1, GATE_BLOCK)
             if d_a % w == 0 and (w == GATE_BLOCK or 6 * 4 * SUBLANES * seg * w <= RGLRU_SCRATCH_BYTES))
    nc = d_a // cw
    nl = cw // LANES
    chan = lambda rows: pl.BlockSpec((None, rows, cw), lambda b, c: (b, 0, c))
    par = lambda rows: pl.BlockSpec((None, rows, cw), lambda b, c: (layer, 0, c))
    wspec = pl.BlockSpec((None, cw // GATE_BLOCK, GATE_BLOCK, GATE_BLOCK), lambda b, c: (layer, c, 0, 0))
    in_specs = [chan(seq), chan(seq), chan(SUBLANES), chan(1), par(CONV_W), par(1),
                wspec, par(1), wspec, par(1), par(1)]
    args = [xa3, ga3, buf8, h0, conv_w, conv_b, wr_bd, br, wi_bd, bi, lam]
    out_specs = [chan(seq), chan(SUBLANES), chan(1)]
    out_shape = [jax.ShapeDtypeStruct((batch, seq, d_a), BF16),
                 jax.ShapeDtypeStruct((batch, SUBLANES, d_a), F32),
                 jax.ShapeDtypeStruct((batch, 1, d_a), F32)]
    steps = batch * nc
    aliases = {}
    for w, src_layer, _ in casts:
        rows = w.shape[1] // steps
        assert rows * steps == w.shape[1] and rows % BF16_ROWS == 0, (w.shape, steps)
        slab = pl.BlockSpec((None, rows, w.shape[2]), lambda b, c, sl=src_layer: (sl, b * nc + c, 0))
        in_specs.append(slab)
        args.append(w)
        out_specs.append(slab)
        out_shape.append(jax.ShapeDtypeStruct(w.shape, BF16))
    for k, (_, _, stack) in enumerate(casts):
        if stack is not None:
            aliases[len(args)] = 3 + k
            in_specs.append(pl.BlockSpec(memory_space=pl.ANY))
            args.append(stack)
    y, nbuf, hl, *filled = pl.pallas_call(
        functools.partial(_rglru_kernel, seq=seq, seg=seg, g_block=g_block, n_cast=len(casts)),
        grid=(batch, nc),
        in_specs=in_specs,
        out_specs=out_specs,
        out_shape=out_shape,
        input_output_aliases=aliases,
        scratch_shapes=[pltpu.VMEM((nl, SUBLANES * (seg + 1), LANES), F32),
                        pltpu.VMEM((SUBLANES * seg, cw), F32),
                        pltpu.VMEM((SUBLANES * seg, cw), F32),
                        pltpu.VMEM((SUBLANES * seg, cw), F32),
                        pltpu.VMEM((SUBLANES * seg, cw), F32),
                        pltpu.VMEM((nl, SUBLANES * seg, LANES), F32)],
        compiler_params=pltpu.CompilerParams(
            dimension_semantics=("arbitrary", "arbitrary"), vmem_limit_bytes=VMEM_LIMIT_BYTES),
        name="rglru_mixer",
    )(*args)
    return (y.reshape(m, d_a), nbuf[:, SUBLANES - (CONV_W - 1):, :], hl[:, 0, :], *filled)


def _diff_lambda(lv, lam_init):
    s1 = jnp.sum(lv[0:1, :] * lv[1:2, :], axis=-1, keepdims=True)
    s2 = jnp.sum(lv[2:3, :] * lv[3:4, :], axis=-1, keepdims=True)
    return jnp.exp(s1) - jnp.exp(s2) + lam_init


def _attn_prompt_kernel(lv_ref, q_ref, k_ref, v_ref, gb_ref, bt_ref, g_ref, o_ref,
                        q1t_s, q2t_s, vt_s, acc_s, s_s, *, seq, n_tiles, lam_init):
    gap = LANES - N_META
    lam = _diff_lambda(lv_ref[...], lam_init)

    row = lax.broadcasted_iota(jnp.int32, (2 * DK, LANES), 0)
    zero = jnp.zeros((2 * DK, LANES), BF16)
    pad = jnp.zeros((gap, DV), BF16)
    for j in range(2 * n_tiles + 1):
        cols = slice(LANES * j, LANES * (j + 1))
        if j == 0:
            qb = jnp.concatenate([q_ref[0:N_META, :], pad], axis=0)
            vb = jnp.concatenate([v_ref[0:N_META, :], pad], axis=0)
        else:
            rows = slice(N_META + LANES * (j - 1), N_META + LANES * j)
            qb = q_ref[rows, :]
            vb = v_ref[rows, :]
        qt = qb.T
        q1t_s[:, cols] = jnp.where(row < DK, qt, zero)
        q2t_s[:, cols] = jnp.where(row >= DK, qt, zero)
        vt_s[:, cols] = vb.T

    gcol = g_ref[...] * (1.0 - lam_init)
    meta_pad = jnp.zeros((gap, Q_TILE), BF16)

    def first_main(i):
        return Q_TILE * (i + 1) - KEY_BLOCK * ((Q_TILE * (i + 1) - 1) // KEY_BLOCK)

    tiles = [(0, LANES, 0, seq, 0, N_META)]
    for i in range(n_tiles):
        tiles.append((LANES + Q_TILE * i, Q_TILE, 1 + (Q_TILE * (i + 1) - first_main(i)) // KEY_BLOCK,
                      Q_TILE * (n_tiles - 1 - i), N_META + Q_TILE * i, Q_TILE))
    q_maps = (q1t_s, q2t_s)
    state = {}

    def key_rows(t, j):
        first = first_main(t - 1) if t else 0
        if j == 0:
            return 0, N_META + first
        return N_META + first + KEY_BLOCK * (j - 1), KEY_BLOCK

    def scores(t, j, slot):
        q_col, n_q, _, b_lo, _, _ = tiles[t]
        r0, nr = key_rows(t, j)
        far = t > 0 and r0 + nr - 1 - tiles[t][4] <= -FAR_REL
        cms = []
        for m in range(2):
            s = jnp.dot(k_ref[r0:r0 + nr, :], q_maps[m][:, q_col:q_col + n_q],
                        preferred_element_type=F32)
            if not far:
                s = s + bt_ref[b_lo + r0:b_lo + r0 + nr, 0:n_q]
            s_s[slot, m, 0:nr, 0:n_q] = s
            cms.append(jnp.max(s, axis=0, keepdims=True))
        return cms

    def fold(t, j, slot, cms):
        _, n_q, _, _, _, _ = tiles[t]
        r0, nr = key_rows(t, j)
        ops = []
        for m in range(2):
            if j == 0:
                m_new, alpha = cms[m], None
                s = s_s[slot, m, 0:nr, 0:n_q]
                p = jnp.exp2(s - m_new)
                den = jnp.sum(p, axis=0, keepdims=True)
                pb = p.astype(BF16)
                parts = [pb[0:N_META], meta_pad[:, 0:n_q]]
                if nr > N_META:
                    parts.append(pb[N_META:nr])
                pb = jnp.concatenate(parts, axis=0)
                c0, nc = 0, LANES + nr - N_META
            else:
                mx, den = state[t, m]
                m_new = jnp.maximum(mx, cms[m])
                alpha = jnp.exp2(mx - m_new)
                p = jnp.exp2(s_s[slot, m, 0:nr, 0:n_q] - m_new)
                den = alpha * den + jnp.sum(p, axis=0, keepdims=True)
                pb = p.astype(BF16)
                c0, nc = LANES + r0 - N_META, nr
            state[t, m] = (m_new, den)
            ops.append((alpha, pb, c0, nc))
        for m, (alpha, pb, c0, nc) in enumerate(ops):
            o = jnp.dot(vt_s[:, c0:c0 + nc], pb, preferred_element_type=F32)
            if alpha is None:
                acc_s[t, m, :, 0:n_q] = o
            else:
                acc_s[t, m, :, 0:n_q] = alpha * acc_s[t, m, :, 0:n_q] + o

    def finish(t):
        _, n_q, _, _, q_lo, out_rows = tiles[t]
        outs = [acc_s[t, m, :, 0:n_q] * (1.0 / state[t, m][1]) for m in range(2)]
        ot = outs[0] - lam * outs[1]
        ms = jnp.mean(ot * ot, axis=0, keepdims=True)
        on = (ot * lax.rsqrt(ms + EPS)) * gcol
        o_rows = on.T[0:out_rows, :]
        gb = gb_ref[q_lo:q_lo + out_rows, :].astype(F32)
        o_ref[q_lo:q_lo + out_rows, :] = (o_rows * (gb * _sigmoid(gb))).astype(o_ref.dtype)

    jobs = [(0, 0)] + [(t, j) for _, t, j in sorted(
        ((j + 0.5) / tiles[t][2], t, j) for t in range(1, len(tiles)) for j in range(tiles[t][2]))]
    cms_next = scores(*jobs[0], 0)
    for n, (t, j) in enumerate(jobs):
        cms = cms_next
        if n + 1 < len(jobs):
            cms_next = scores(*jobs[n + 1], (n + 1) % 2)
        fold(t, j, n % 2, cms)
        if j == max(tiles[t][2] - 1, 0):
            finish(t)


def _attn_prompt(q, kb, vb, gb, slab, lam_vec, subln_col, layer, lam_init, batch, seq):
    m, width = q.shape
    n_tiles = (seq - N_META) // Q_TILE
    assert N_META + n_tiles * Q_TILE == seq
    tp = (2 * n_tiles + 1) * LANES
    to3 = lambda a: a.reshape(batch, seq, width)
    head = pl.BlockSpec((None, seq, DV), lambda h, b: (b, 0, h))
    out = pl.pallas_call(
        functools.partial(_attn_prompt_kernel, seq=seq, n_tiles=n_tiles, lam_init=lam_init),
        grid=(N_HEADS_B, batch),
        in_specs=[pl.BlockSpec((None, 4, DK), lambda h, b: (layer, 0, 0)),
                  head, head, head, head,
                  pl.BlockSpec((None, seq + N_META, Q_TILE), lambda h, b: (h, 0, 0)),
                  pl.BlockSpec((None, DV, 1), lambda h, b: (layer, 0, 0))],
        out_specs=head,
        out_shape=jax.ShapeDtypeStruct((batch, seq, width), BF16),
        scratch_shapes=[pltpu.VMEM((2 * DK, tp), BF16),
                        pltpu.VMEM((2 * DK, tp), BF16),
                        pltpu.VMEM((DV, tp), BF16),
                        pltpu.VMEM((n_tiles + 1, 2, DV, Q_TILE), F32),
                        pltpu.VMEM((2, 2, N_META + KEY_BLOCK, Q_TILE), F32)],
        compiler_params=pltpu.CompilerParams(
            dimension_semantics=("arbitrary", "arbitrary"), vmem_limit_bytes=VMEM_LIMIT_BYTES),
        name="diff_attention_prompt",
    )(lam_vec, to3(q), to3(kb), to3(vb), to3(gb), slab, subln_col)
    return out.reshape(m, width)


def _attn_sample_kernel(lv_ref, q_ref, kc_ref, vc_ref, kn_ref, vn_ref, gb_ref, bs_ref, g_ref, o_ref,
                        kb_s, vb_s, s_s, *, past, dec_seq, lam_init):
    lam = _diff_lambda(lv_ref[...], lam_init)
    width = past + LANES
    lanes = lambda h: slice(h * DV, (h + 1) * DV)
    lane = lax.broadcasted_iota(jnp.int32, (dec_seq, DV), 1)
    zero = jnp.zeros((dec_seq, DV), BF16)
    tail = jnp.zeros((LANES - dec_seq, DV), BF16)

    def scores(h, slot):
        for dst, cache, new in ((kb_s, kc_ref, kn_ref), (vb_s, vc_ref, vn_ref)):
            dst[slot, 0:past, :] = cache[pl.ds(h, past, stride=N_HEADS_B), :].astype(BF16)
            dst[slot, past:past + dec_seq, :] = new[:, lanes(h)]
            dst[slot, past + dec_seq:width, :] = tail
        qv = q_ref[:, lanes(h)]
        q2 = jnp.concatenate([jnp.where(lane < DK, qv, zero), jnp.where(lane >= DK, qv, zero)], axis=0)
        s = lax.dot_general(q2, kb_s[slot], (((1,), (1,)), ((), ())), preferred_element_type=F32)
        for m in range(2):
            s_s[slot, m] = s[m * dec_seq:(m + 1) * dec_seq] + bs_ref[h]

    def fold(h, slot):
        probs = []
        for m in range(2):
            s = s_s[slot, m]
            mx = jnp.max(s, axis=-1, keepdims=True)
            p = jnp.exp2(s - mx)
            probs.append(p * (1.0 / jnp.sum(p, axis=-1, keepdims=True)))
        pd = (probs[0] - lam * probs[1]).astype(BF16)
        o = jnp.dot(pd, vb_s[slot], preferred_element_type=F32)
        ms = jnp.mean(o * o, axis=-1, keepdims=True)
        on = ((o * lax.rsqrt(ms + EPS)) * g_ref[...]) * (1.0 - lam_init)
        gb = gb_ref[:, lanes(h)].astype(F32)
        o_ref[:, lanes(h)] = (on * (gb * _sigmoid(gb))).astype(o_ref.dtype)

    scores(0, 0)
    for h in range(N_HEADS_B):
        if h + 1 < N_HEADS_B:
            scores(h + 1, (h + 1) % 2)
        fold(h, h % 2)


def _attn_sample(q, kb, vb, gb, cache_k, cache_v, slab, lam_vec, subln_row, layer, lam_init,
                 batch, dec_seq):
    m, width = q.shape
    depth, _, past = cache_k.shape[:3]
    ck = cache_k.reshape(depth, batch, past * N_HEADS_B, DV)
    cv = cache_v.reshape(depth, batch, past * N_HEADS_B, DV)
    rows = pl.BlockSpec((dec_seq, width), lambda b: (b, 0))
    cache = pl.BlockSpec((None, None, past * N_HEADS_B, DV), lambda b: (layer, b, 0, 0))
    return pl.pallas_call(
        functools.partial(_attn_sample_kernel, past=past, dec_seq=dec_seq, lam_init=lam_init),
        grid=(batch,),
        in_specs=[pl.BlockSpec((None, 4, DK), lambda b: (layer, 0, 0)),
                  rows, cache, cache, rows, rows, rows,
                  pl.BlockSpec((N_HEADS_B, dec_seq, past + LANES), lambda b: (0, 0, 0)),
                  pl.BlockSpec((None, 1, DV), lambda b: (layer, 0, 0))],
        out_specs=rows,
        out_shape=jax.ShapeDtypeStruct((m, width), BF16),
        scratch_shapes=[pltpu.VMEM((2, past + LANES, DV), BF16),
                        pltpu.VMEM((2, past + LANES, DV), BF16),
                        pltpu.VMEM((2, 2, dec_seq, past + LANES), F32)],
        compiler_params=pltpu.CompilerParams(
            dimension_semantics=("arbitrary",), vmem_limit_bytes=VMEM_LIMIT_BYTES),
        name="diff_attention_sample",
    )(lam_vec, q, ck, cv, kb, vb, gb, slab, subln_row)


def _outproj_kernel(ya_ref, yb_ref, x_ref, w_ref, g_ref, o_ref, y_s, *, col_chunk):
    d_a = ya_ref.shape[-1]
    d = o_ref.shape[-1]
    for c in range(0, d, col_chunk):
        y_s[:, c:c + col_chunk] = (
            jnp.dot(ya_ref[...], w_ref[0:d_a, c:c + col_chunk], preferred_element_type=F32)
            + jnp.dot(yb_ref[...], w_ref[d_a:, c:c + col_chunk], preferred_element_type=F32))
    y = y_s[...]
    ms = jnp.mean(y * y, axis=-1, keepdims=True)
    o_ref[...] = x_ref[...] + (y * lax.rsqrt(ms + EPS)) * g_ref[...]


def _outproj(ya, yb, x2d, w_out_b, post_g, layer):
    m, d = x2d.shape
    tm = _row_tile(m, 384)
    half = pl.BlockSpec((tm, d // 2), lambda i: (i, 0))
    full = pl.BlockSpec((tm, d), lambda i: (i, 0))
    return pl.pallas_call(
        functools.partial(_outproj_kernel, col_chunk=PROJ_COLS),
        grid=(m // tm,),
        in_specs=[half, half, full,
                  pl.BlockSpec((None, d, d), lambda i: (layer, 0, 0)),
                  pl.BlockSpec((None, 1, d), lambda i: (layer, 0, 0))],
        out_specs=full,
        out_shape=jax.ShapeDtypeStruct((m, d), F32),
        scratch_shapes=[pltpu.VMEM((tm, d), F32)],
        compiler_params=pltpu.CompilerParams(
            dimension_semantics=("arbitrary",), vmem_limit_bytes=VMEM_LIMIT_BYTES),
        name="out_projection",
    )(ya, yb, x2d, w_out_b, post_g)


def _outproj_main_kernel(ya_ref, yb_ref, x_ref, w_ref, g_ref, o_ref, y_s, *, col_chunk):
    _outproj_kernel(ya_ref.at[0], yb_ref.at[0], x_ref.at[0], w_ref, g_ref, o_ref, y_s,
                    col_chunk=col_chunk)


def _outproj_main(ya, yb, x2d, w_out_b, post_g, layer, batch, seq):
    m, d = x2d.shape
    main = seq - N_META
    tm = _row_tile(main, 512)
    assert tm % N_META == 0
    row0 = lambda j: (j * (tm // N_META) + 1) * N_META
    win = lambda w: pl.BlockSpec((pl.Element(1), pl.Element(tm), pl.Element(w)),
                                 lambda b, j: (b, row0(j), 0))
    return pl.pallas_call(
        functools.partial(_outproj_main_kernel, col_chunk=PROJ_COLS),
        grid=(batch, main // tm),
        in_specs=[win(d // 2), win(d // 2), win(d),
                  pl.BlockSpec((None, d, d), lambda b, j: (layer, 0, 0)),
                  pl.BlockSpec((None, 1, d), lambda b, j: (layer, 0, 0))],
        out_specs=pl.BlockSpec((None, tm, d), lambda b, j: (b, j, 0)),
        out_shape=jax.ShapeDtypeStruct((batch, main, d), F32),
        scratch_shapes=[pltpu.VMEM((tm, d), F32)],
        compiler_params=pltpu.CompilerParams(
            dimension_semantics=("arbitrary", "arbitrary"), vmem_limit_bytes=VMEM_LIMIT_BYTES),
        name="out_projection_main",
    )(ya.reshape(batch, seq, d // 2), yb.reshape(batch, seq, d // 2), x2d.reshape(batch, seq, d),
      w_out_b, post_g)


def _block_diag_gates(w):
    depth, nb, bs, _ = w.shape
    per = GATE_BLOCK // bs
    w5 = w.reshape(depth, nb // per, per, bs, bs)
    eye = jnp.eye(per, dtype=w.dtype)
    bd = jnp.einsum('lgaij,ab->lgaibj', w5, eye)
    return bd.reshape(depth, nb // per, GATE_BLOCK, GATE_BLOCK).astype(BF16)


def kernel(x_prompt, x_sample, cache_k, cache_v, state_conv, state_rglru, meta, rel_bias, pre_g, post_g,
           w_in, conv_w, conv_b, gate_r_w, gate_r_b, gate_i_w, gate_i_b, rglru_lam, lam_q1, lam_k1,
           lam_q2, lam_k2, subln_g, w_out):
    batch, seq0, d = x_prompt.shape
    dec_batch, dec_seq, _ = x_sample.shape
    depth = w_in.shape[0]
    past = cache_k.shape[2]
    d_a = d // 2
    seq = seq0 + N_META
    n_tiles = seq0 // Q_TILE

    hp = jnp.concatenate(
        [jnp.broadcast_to(meta.astype(x_prompt.dtype)[None], (batch, N_META, d)), x_prompt],
        axis=1).reshape(batch * seq, d)
    hs = x_sample.reshape(dec_batch * dec_seq, d)

    w_in_first = w_in[0:1].astype(BF16)
    w_in_b = None
    w_out_b = None
    wr_bd = _block_diag_gates(gate_r_w)
    wi_bd = _block_diag_gates(gate_i_w)
    pre_g3 = pre_g.reshape(depth, 1, d)
    post_g3 = post_g.reshape(depth, 1, d)
    conv_b3 = conv_b.reshape(depth, 1, d_a)
    br3 = gate_r_b.reshape(depth, 1, d_a)
    bi3 = gate_i_b.reshape(depth, 1, d_a)
    lam3 = rglru_lam.reshape(depth, 1, d_a)
    lam_vec = jnp.stack([lam_q1, lam_k1, lam_q2, lam_k2], axis=1)
    subln_col = subln_g.reshape(depth, DV, 1)
    subln_row = subln_g.reshape(depth, 1, DV)

    slab_p, slab_s = _bias_slabs(rel_bias, n_tiles, past, dec_seq)

    zero_buf = jnp.zeros((batch, SUBLANES, d_a), F32)
    zero_h = jnp.zeros((batch, 1, d_a), F32)
    pad_rows = SUBLANES - (CONV_W - 1)
    state_conv8 = jnp.pad(state_conv, ((0, 0), (0, 0), (pad_rows, 0), (0, 0)))

    kv_p = None
    kv_s = None
    cp_l, rp_l, cs_l, rs_l = [], [], [], []
    for l in range(depth):
        lam_init = 0.8 - 0.6 * math.exp(-0.3 * l)
        w_l = (w_in_first, 0) if l == 0 else (w_in_b, l)
        xa, ga, q, kb, vb, gb, k5, v5 = _inproj(hp, pre_g3, *w_l, l, kv_p, depth)
        kv_p = (k5, v5)
        casts = [(w_out, l, w_out_b)] + ([(w_in, l + 1, w_in_b)] if l + 1 < depth else [])
        ya, cp, rp, *cast = _rglru(xa, ga, zero_buf, zero_h, conv_w, conv_b3, wr_bd, br3, wi_bd, bi3,
                                   lam3, l, batch, seq, casts=casts)
        w_out_b = cast[0]
        if l + 1 < depth:
            w_in_b = cast[1]
        if l:
            w_l = (w_in_b, l)
        yb = _attn_prompt(q, kb, vb, gb, slab_p, lam_vec, subln_col, l, lam_init, batch, seq)
        if l + 1 < depth:
            hp = _outproj(ya, yb, hp, w_out_b, post_g3, l)
        else:
            y_prompt = _outproj_main(ya, yb, hp, w_out_b, post_g3, l, batch, seq)
        cp_l.append(cp)
        rp_l.append(rp)
        xa, ga, q, kb, vb, gb, k5, v5 = _inproj(hs, pre_g3, *w_l, l, kv_s, depth)
        kv_s = (k5, v5)
        ya, cs, rs = _rglru(xa, ga, state_conv8[l], state_rglru[l][:, None, :], conv_w, conv_b3,
                            wr_bd, br3, wi_bd, bi3, lam3, l, dec_batch, dec_seq)
        yb = _attn_sample(q, kb, vb, gb, cache_k, cache_v, slab_s, lam_vec, subln_row, l, lam_init,
                          dec_batch, dec_seq)
        hs = _outproj(ya, yb, hs, w_out_b, post_g3, l)
        cs_l.append(cs)
        rs_l.append(rs)

    y_sample = hs.reshape(dec_batch, dec_seq, d)
    k_prompt = kv_p[0].reshape(depth, batch, seq, N_HEADS_B, 2 * DK)
    v_prompt = kv_p[1].reshape(depth, batch, seq, N_HEADS_B, DV)
    k_sample = kv_s[0].reshape(depth, dec_batch, dec_seq, N_HEADS_B, 2 * DK)
    v_sample = kv_s[1].reshape(depth, dec_batch, dec_seq, N_HEADS_B, DV)
    return (y_prompt, y_sample, k_prompt, v_prompt, jnp.stack(cp_l), jnp.stack(rp_l),
            k_sample, v_sample, jnp.stack(cs_l), jnp.stack(rs_l))
```

```python
import functools
import math

import jax
import jax.numpy as jnp
from jax import lax
from jax.experimental import pallas as pl
from jax.experimental.pallas import tpu as pltpu

F32 = jnp.float32
BF16 = jnp.bfloat16

CHUNK = 64
N_META = 16
N_BLOCKS_A = 16
CONV_W = 4
RG_C = 8.0
N_HEADS_B = 8
DK = 64
DV = 2 * DK
NUM_BUCKETS = 32
REL_MAX_DIST = 1024
EPS = 1e-6

LANES = 128
SUBLANES = 8
BF16_ROWS = 16
VMEM_LIMIT_BYTES = 56 * 1024 * 1024

LOG2E = math.log2(math.e)
Q_TILE = 256
KEY_BLOCK = 2048
PROJ_COLS = 512
FAR_REL = math.ceil((NUM_BUCKETS // 4) * (REL_MAX_DIST / (NUM_BUCKETS // 4))
                    ** ((NUM_BUCKETS // 4 - 1) / (NUM_BUCKETS // 4))) + CHUNK
GATE_BLOCK = 256
RGLRU_SCRATCH_BYTES = 16 * 1024 * 1024


def _sigmoid(x):
    return 1.0 / (1.0 + jnp.exp(-x))


def _row_tile(m, cap=512):
    best = None
    for t in range(BF16_ROWS, min(m, cap) + 1, BF16_ROWS):
        if m % t == 0:
            best = t
    assert best is not None, m
    return best


def _bias_lookup(rel, tab_row):
    half = NUM_BUCKETS // 2
    max_exact = half // 2
    ret = jnp.where(rel > 0, half, 0).astype(jnp.int32)
    n = jnp.abs(rel)
    nf = jnp.maximum(n, 1).astype(F32)
    large = max_exact + (jnp.log(nf / max_exact) / math.log(REL_MAX_DIST / max_exact)
                         * (half - max_exact)).astype(jnp.int32)
    large = jnp.minimum(large, half - 1)
    bucket = ret + jnp.where(n < max_exact, n, large)
    tab = jnp.broadcast_to(tab_row, (rel.shape[0], LANES))
    return jnp.concatenate(
        [jnp.take_along_axis(tab, bucket[:, c:c + LANES], axis=1) for c in range(0, rel.shape[1], LANES)],
        axis=1)


def _prompt_slab_kernel(tab_ref, out_ref, *, n_tiles):
    seq = Q_TILE * n_tiles + N_META
    diag0 = Q_TILE * (n_tiles - 1) + N_META
    far_bucket = NUM_BUCKETS // 2 - 1
    tab_row = tab_ref[...] - tab_ref[:, far_bucket:far_bucket + 1]
    r0 = 0
    while r0 < seq:
        blk = min(LANES, seq - r0)
        r = lax.broadcasted_iota(jnp.int32, (blk, Q_TILE), 0) + r0
        c = lax.broadcasted_iota(jnp.int32, (blk, Q_TILE), 1)
        val = _bias_lookup(r - c - diag0, tab_row) * LOG2E
        rp = r - diag0
        masked = jnp.logical_and(rp >= 0, (rp // CHUNK) > (c // CHUNK))
        out_ref[r0:r0 + blk, :] = jnp.where(masked, -jnp.inf, val)
        r0 += blk
    k = lax.broadcasted_iota(jnp.int32, (N_META, Q_TILE), 0)
    c = lax.broadcasted_iota(jnp.int32, (N_META, Q_TILE), 1)
    out_ref[seq:seq + N_META, :] = _bias_lookup(k - c, tab_row) * LOG2E


def _sample_slab_kernel(tab_ref, out_ref, *, past, dec_seq):
    width = out_ref.shape[-1]
    for c0 in range(0, width, 512):
        w = min(512, width - c0)
        t = lax.broadcasted_iota(jnp.int32, (dec_seq, w), 0)
        k = lax.broadcasted_iota(jnp.int32, (dec_seq, w), 1) + c0
        val = _bias_lookup(k - past - t, tab_ref[...]) * LOG2E
        out_ref[:, c0:c0 + w] = jnp.where(k >= past + dec_seq, -jnp.inf, val)


def _bias_slabs(rel_bias, n_tiles, past, dec_seq):
    rows = Q_TILE * n_tiles + 2 * N_META
    tab = jnp.pad(rel_bias.T, ((0, 0), (0, LANES - NUM_BUCKETS)))[:, None, :]
    tab_spec = pl.BlockSpec((None, 1, LANES), lambda h: (h, 0, 0))
    prompt = pl.pallas_call(
        functools.partial(_prompt_slab_kernel, n_tiles=n_tiles),
        grid=(N_HEADS_B,),
        in_specs=[tab_spec],
        out_specs=pl.BlockSpec((None, rows, Q_TILE), lambda h: (h, 0, 0)),
        out_shape=jax.ShapeDtypeStruct((N_HEADS_B, rows, Q_TILE), F32),
        name="prompt_bias_slab",
    )(tab)
    width = past + LANES
    sample = pl.pallas_call(
        functools.partial(_sample_slab_kernel, past=past, dec_seq=dec_seq),
        grid=(N_HEADS_B,),
        in_specs=[tab_spec],
        out_specs=pl.BlockSpec((None, dec_seq, width), lambda h: (h, 0, 0)),
        out_shape=jax.ShapeDtypeStruct((N_HEADS_B, dec_seq, width), F32),
        name="sample_bias_slab",
    )(tab)
    return prompt, sample


def _inproj_kernel(x_ref, g_ref, w_ref, *rest, col_chunk):
    xa_ref, ga_ref, q_ref, kb_ref, vb_ref, gb_ref, k5_ref, v5_ref, u_s = rest[-9:]
    tm = x_ref.shape[0]

    def normalise(lo, n):
        x = x_ref[lo:lo + n, :]
        ms = jnp.mean(x * x, axis=-1, keepdims=True)
        u_s[lo:lo + n, :] = ((x * lax.rsqrt(ms + EPS)) * g_ref[...]).astype(BF16)

    chunks = []
    col = 0
    for ref, scale, ref5 in ((xa_ref, None, None), (ga_ref, None, None),
                             (q_ref, (DK ** -0.5) * LOG2E, None),
                             (kb_ref, None, k5_ref), (vb_ref, None, v5_ref), (gb_ref, None, None)):
        for c in range(0, ref.shape[-1], col_chunk):
            chunks.append((ref, scale, ref5, col + c, c))
        col += ref.shape[-1]

    def project(lo, n, chunk):
        ref, scale, ref5, wcol, c = chunk
        acc = jnp.dot(u_s[lo:lo + n, :], w_ref[:, wcol:wcol + col_chunk], preferred_element_type=F32)
        if scale is not None:
            acc = acc * scale
        ref[lo:lo + n, c:c + col_chunk] = acc.astype(ref.dtype)
        if ref5 is not None:
            for j in range(col_chunk // DV):
                head = c // DV + j
                ref5[pl.ds(lo * N_HEADS_B + head, n, stride=N_HEADS_B), :] = acc[:, j * DV:(j + 1) * DV]

    normalise(0, tm)
    for chunk in chunks:
        project(0, tm, chunk)


def _inproj(x2d, pre_g, w_in_b, w_layer, layer, kv_stacks, depth):
    m, d = x2d.shape
    d_a = d // 2
    width = N_HEADS_B * DV
    in_cols = w_in_b.shape[-1]
    tm = _row_tile(m, 384)
    row = lambda w: pl.BlockSpec((tm, w), lambda i: (i, 0))
    stack = pl.BlockSpec((None, tm * N_HEADS_B, DV), lambda i: (layer, i, 0))
    in_specs = [
        row(d),
        pl.BlockSpec((None, 1, d), lambda i: (layer, 0, 0)),
        pl.BlockSpec((None, d, in_cols), lambda i: (w_layer, 0, 0), pipeline_mode=pl.Buffered(1)),
    ]
    args = [x2d, pre_g, w_in_b]
    aliases = {}
    if kv_stacks is not None:
        in_specs += [pl.BlockSpec(memory_space=pl.ANY)] * 2
        args += list(kv_stacks)
        aliases = {3: 6, 4: 7}
    out_shape = [
        jax.ShapeDtypeStruct((m, d_a), F32),
        jax.ShapeDtypeStruct((m, d_a), F32),
        jax.ShapeDtypeStruct((m, width), BF16),
        jax.ShapeDtypeStruct((m, width), BF16),
        jax.ShapeDtypeStruct((m, width), BF16),
        jax.ShapeDtypeStruct((m, width), BF16),
        jax.ShapeDtypeStruct((depth, m * N_HEADS_B, DV), F32),
        jax.ShapeDtypeStruct((depth, m * N_HEADS_B, DV), F32),
    ]
    out_specs = [row(d_a), row(d_a), row(width), row(width), row(width), row(width), stack, stack]
    return pl.pallas_call(
        functools.partial(_inproj_kernel, col_chunk=PROJ_COLS),
        grid=(m // tm,),
        in_specs=in_specs,
        out_specs=out_specs,
        out_shape=out_shape,
        scratch_shapes=[pltpu.VMEM((tm, d), BF16)],
        input_output_aliases=aliases,
        compiler_params=pltpu.CompilerParams(
            dimension_semantics=("arbitrary",), vmem_limit_bytes=VMEM_LIMIT_BYTES),
        name="in_projection",
    )(*args)


def _rglru_kernel(xa_ref, ga_ref, buf_ref, h0_ref, cw_ref, cb_ref, wr_ref, br_ref, wi_ref, bi_ref,
                  lam_ref, *rest, seq, seg, g_block, n_cast):
    xp_s, a_s, b_s, hs_s, as_s, hn_s = rest[-6:]
    outs = rest[-6 - 3 - n_cast:-6]
    y_ref, nbuf_ref, hl_ref = outs[:3]
    for src_ref, dst_ref in zip(rest[:n_cast], outs[3:]):
        dst_ref[...] = src_ref[...].astype(dst_ref.dtype)
    c = xa_ref.shape[-1]
    nl = c // LANES
    lanes = lambda j: slice(j * LANES, (j + 1) * LANES)
    pad = SUBLANES * seg - seq
    for j in range(nl):
        xp_s[j, 0:SUBLANES, :] = buf_ref[:, lanes(j)]
        xp_s[j, SUBLANES:SUBLANES + seq, :] = xa_ref[:, lanes(j)]
        if pad:
            xp_s[j, SUBLANES + seq:, :] = jnp.zeros((pad, LANES), F32)
        nbuf_ref[:, lanes(j)] = xp_s[j, seq:seq + SUBLANES, :]

    neg = -lam_ref[...]
    softplus = jnp.maximum(neg, 0.0) + jnp.log1p(jnp.exp(-jnp.abs(neg)))
    cneg = -RG_C * softplus
    cb = cb_ref[...]
    taps = [cw_ref[j:j + 1, :] for j in range(CONV_W)]

    cache = {}

    def rows_at(g):
        if g not in cache:
            cache[g] = jnp.concatenate(
                [xp_s[j, pl.ds(SUBLANES + g, SUBLANES, stride=seg), :] for j in range(nl)], axis=1)
        return cache[g]

    for g0 in range(0, seg, g_block):
        pieces = []
        for g in range(g0, g0 + g_block):
            xc = cb
            for j in range(CONV_W):
                xc = xc + rows_at(g - (CONV_W - 1) + j) * taps[j]
            pieces.append(xc)
        xc = jnp.concatenate(pieces, axis=0)
        xcb = xc.astype(BF16)

        def gate(w_ref, b_ref):
            z = [jnp.dot(xcb[:, k * GATE_BLOCK:(k + 1) * GATE_BLOCK], w_ref[k],
                         preferred_element_type=F32) for k in range(c // GATE_BLOCK)]
            return _sigmoid(jnp.concatenate(z, axis=1) + b_ref[...])

        r = gate(wr_ref, br_ref)
        ig = gate(wi_ref, bi_ref)
        log_a = r * cneg
        a = jnp.exp(log_a)
        lo, hi = g0 * SUBLANES, (g0 + g_block) * SUBLANES
        a_s[lo:hi, :] = a
        e = -jnp.tanh(log_a) * (a * a + 1.0)
        root = jnp.where(e > 0.0, e * lax.rsqrt(e), 0.0)
        b_s[lo:hi, :] = root * (ig * xc)

    rowid = lax.broadcasted_iota(jnp.int32, (SUBLANES, c), 0)
    first = rowid == 0

    def scan_body(g, carry):
        h, acc = carry
        r = pl.multiple_of(g * SUBLANES, SUBLANES)
        a = a_s[pl.ds(r, SUBLANES), :]
        h = a * h + b_s[pl.ds(r, SUBLANES), :]
        acc = a * acc
        hs_s[pl.ds(r, SUBLANES), :] = h
        as_s[pl.ds(r, SUBLANES), :] = acc
        return h, acc

    h_init = jnp.where(first, jnp.broadcast_to(h0_ref[...], (SUBLANES, c)), 0.0)
    end_h, end_a = lax.fori_loop(0, seg, scan_body, (h_init, jnp.ones((SUBLANES, c), F32)), unroll=8)

    d = jnp.zeros((SUBLANES, c), F32)
    for _ in range(SUBLANES - 1):
        d = jnp.where(first, 0.0, pltpu.roll(end_h + end_a * d, 1, 0))

    def fix_body(g, carry):
        r = pl.multiple_of(g * SUBLANES, SUBLANES)
        h = hs_s[pl.ds(r, SUBLANES), :] + as_s[pl.ds(r, SUBLANES), :] * d
        for j in range(nl):
            hn_s[j, pl.ds(g, SUBLANES, stride=seg), :] = h[:, lanes(j)]
        return carry

    lax.fori_loop(0, seg, fix_body, 0, unroll=8)

    for j in range(nl):
        ga = ga_ref[:, lanes(j)]
        y_ref[:, lanes(j)] = (hn_s[j, 0:seq, :] * (ga * _sigmoid(ga))).astype(y_ref.dtype)
        hl_ref[:, lanes(j)] = hn_s[j, seq - 1:seq, :]


def _rglru(xa, ga, buf8, h0, conv_w, conv_b, wr_bd, br, wi_bd, bi, lam, layer, batch, seq, casts=()):
    m, d_a = xa.shape
    xa3 = xa.reshape(batch, seq, d_a)
    ga3 = ga.reshape(batch, seq, d_a)
    seg = pl.cdiv(pl.cdiv(seq, SUBLANES), SUBLANES) * SUBLANES
    g_block = max(t for t in range(1, min(seg, 44) + 1) if seg % t == 0)
    cw = max(w for w in range(GATE_BLOCK, d_a + 1, GATE_BLOCK)
             if d_a % w == 0 and (w == GATE_BLOCK or 6 * 4 * SUBLANES * seg * w <= RGLRU_SCRATCH_BYTES))
    nc = d_a // cw
    nl = cw // LANES
    chan = lambda rows: pl.BlockSpec((None, rows, cw), lambda b, c: (b, 0, c))
    par = lambda rows: pl.BlockSpec((None, rows, cw), lambda b, c: (layer, 0, c))
    wspec = pl.BlockSpec((None, cw // GATE_BLOCK, GATE_BLOCK, GATE_BLOCK), lambda b, c: (layer, c, 0, 0))
    in_specs = [chan(seq), chan(seq), chan(SUBLANES), chan(1), par(CONV_W), par(1),
                wspec, par(1), wspec, par(1), par(1)]
    args = [xa3, ga3, buf8, h0, conv_w, conv_b, wr_bd, br, wi_bd, bi, lam]
    out_specs = [chan(seq), chan(SUBLANES), chan(1)]
    out_shape = [jax.ShapeDtypeStruct((batch, seq, d_a), BF16),
                 jax.ShapeDtypeStruct((batch, SUBLANES, d_a), F32),
                 jax.ShapeDtypeStruct((batch, 1, d_a), F32)]
    steps = batch * nc
    aliases = {}
    for w, src_layer, _ in casts:
        rows = w.shape[1] // steps
        assert rows * steps == w.shape[1] and rows % BF16_ROWS == 0, (w.shape, steps)
        slab = pl.BlockSpec((None, rows, w.shape[2]), lambda b, c, sl=src_layer: (sl, b * nc + c, 0))
        in_specs.append(slab)
        args.append(w)
        out_specs.append(slab)
        out_shape.append(jax.ShapeDtypeStruct(w.shape, BF16))
    for k, (_, _, stack) in enumerate(casts):
        if stack is not None:
            aliases[len(args)] = 3 + k
            in_specs.append(pl.BlockSpec(memory_space=pl.ANY))
            args.append(stack)
    y, nbuf, hl, *filled = pl.pallas_call(
        functools.partial(_rglru_kernel, seq=seq, seg=seg, g_block=g_block, n_cast=len(casts)),
        grid=(batch, nc),
        in_specs=in_specs,
        out_specs=out_specs,
        out_shape=out_shape,
        input_output_aliases=aliases,
        scratch_shapes=[pltpu.VMEM((nl, SUBLANES * (seg + 1), LANES), F32),
                        pltpu.VMEM((SUBLANES * seg, cw), F32),
                        pltpu.VMEM((SUBLANES * seg, cw), F32),
                        pltpu.VMEM((SUBLANES * seg, cw), F32),
                        pltpu.VMEM((SUBLANES * seg, cw), F32),
                        pltpu.VMEM((nl, SUBLANES * seg, LANES), F32)],
        compiler_params=pltpu.CompilerParams(
            dimension_semantics=("arbitrary", "arbitrary"), vmem_limit_bytes=VMEM_LIMIT_BYTES),
        name="rglru_mixer",
    )(*args)
    return (y.reshape(m, d_a), nbuf[:, SUBLANES - (CONV_W - 1):, :], hl[:, 0, :], *filled)


def _diff_lambda(lv, lam_init):
    s1 = jnp.sum(lv[0:1, :] * lv[1:2, :], axis=-1, keepdims=True)
    s2 = jnp.sum(lv[2:3, :] * lv[3:4, :], axis=-1, keepdims=True)
    return jnp.exp(s1) - jnp.exp(s2) + lam_init


def _attn_prompt_kernel(lv_ref, q_ref, k_ref, v_ref, gb_ref, bt_ref, g_ref, o_ref,
                        q1t_s, q2t_s, vt_s, acc_s, s_s, *, seq, n_tiles, lam_init):
    gap = LANES - N_META
    lam = _diff_lambda(lv_ref[...], lam_init)

    row = lax.broadcasted_iota(jnp.int32, (2 * DK, LANES), 0)
    zero = jnp.zeros((2 * DK, LANES), BF16)
    pad = jnp.zeros((gap, DV), BF16)
    for j in range(2 * n_tiles + 1):
        cols = slice(LANES * j, LANES * (j + 1))
        if j == 0:
            qb = jnp.concatenate([q_ref[0:N_META, :], pad], axis=0)
            vb = jnp.concatenate([v_ref[0:N_META, :], pad], axis=0)
        else:
            rows = slice(N_META + LANES * (j - 1), N_META + LANES * j)
            qb = q_ref[rows, :]
            vb = v_ref[rows, :]
        qt = qb.T
        q1t_s[:, cols] = jnp.where(row < DK, qt, zero)
        q2t_s[:, cols] = jnp.where(row >= DK, qt, zero)
        vt_s[:, cols] = vb.T

    gcol = g_ref[...] * (1.0 - lam_init)
    meta_pad = jnp.zeros((gap, Q_TILE), BF16)

    def first_main(i):
        return Q_TILE * (i + 1) - KEY_BLOCK * ((Q_TILE * (i + 1) - 1) // KEY_BLOCK)

    tiles = [(0, LANES, 0, seq, 0, N_META)]
    for i in range(n_tiles):
        tiles.append((LANES + Q_TILE * i, Q_TILE, 1 + (Q_TILE * (i + 1) - first_main(i)) // KEY_BLOCK,
                      Q_TILE * (n_tiles - 1 - i), N_META + Q_TILE * i, Q_TILE))
    q_maps = (q1t_s, q2t_s)
    state = {}

    def key_rows(t, j):
        first = first_main(t - 1) if t else 0
        if j == 0:
            return 0, N_META + first
        return N_META + first + KEY_BLOCK * (j - 1), KEY_BLOCK

    def scores(t, j, slot):
        q_col, n_q, _, b_lo, _, _ = tiles[t]
        r0, nr = key_rows(t, j)
        n_far = 0
        if t > 0:
            n_far = min(nr, max(0, tiles[t][4] - FAR_REL + 1 - r0)) // BF16_ROWS * BF16_ROWS
        cms = []
        for m in range(2):
            s = jnp.dot(k_ref[r0:r0 + nr, :], q_maps[m][:, q_col:q_col + n_q],
                        preferred_element_type=F32)
            parts = []
            if n_far:
                parts.append(s[0:n_far])
                s_s[slot, m, 0:n_far, 0:n_q] = parts[-1]
            if n_far < nr:
                parts.append(s[n_far:nr] + bt_ref[b_lo + r0 + n_far:b_lo + r0 + nr, 0:n_q])
                s_s[slot, m, n_far:nr, 0:n_q] = parts[-1]
            cm = [jnp.max(p, axis=0, keepdims=True) for p in parts]
            cms.append(cm[0] if len(cm) == 1 else jnp.maximum(cm[0], cm[1]))
        return cms

    def fold(t, j, slot, cms):
        _, n_q, _, _, _, _ = tiles[t]
        r0, nr = key_rows(t, j)
        ops = []
        for m in range(2):
            if j == 0:
                m_new, alpha = cms[m], None
                s = s_s[slot, m, 0:nr, 0:n_q]
                p = jnp.exp2(s - m_new)
                den = jnp.sum(p, axis=0, keepdims=True)
                pb = p.astype(BF16)
                parts = [pb[0:N_META], meta_pad[:, 0:n_q]]
                if nr > N_META:
                    parts.append(pb[N_META:nr])
                pb = jnp.concatenate(parts, axis=0)
                c0, nc = 0, LANES + nr - N_META
            else:
                mx, den = state[t, m]
                m_new = jnp.maximum(mx, cms[m])
                alpha = jnp.exp2(mx - m_new)
                p = jnp.exp2(s_s[slot, m, 0:nr, 0:n_q] - m_new)
                den = alpha * den + jnp.sum(p, axis=0, keepdims=True)
                pb = p.astype(BF16)
                c0, nc = LANES + r0 - N_META, nr
            state[t, m] = (m_new, den)
            ops.append((alpha, pb, c0, nc))
        for m, (alpha, pb, c0, nc) in enumerate(ops):
            o = jnp.dot(vt_s[:, c0:c0 + nc], pb, preferred_element_type=F32)
            if alpha is None:
                acc_s[t, m, :, 0:n_q] = o
            else:
                acc_s[t, m, :, 0:n_q] = alpha * acc_s[t, m, :, 0:n_q] + o

    def finish(t):
        _, n_q, _, _, q_lo, out_rows = tiles[t]
        outs = [acc_s[t, m, :, 0:n_q] * (1.0 / state[t, m][1]) for m in range(2)]
        ot = outs[0] - lam * outs[1]
        ms = jnp.mean(ot * ot, axis=0, keepdims=True)
        on = (ot * lax.rsqrt(ms + EPS)) * gcol
        o_rows = on.T[0:out_rows, :]
        gb = gb_ref[q_lo:q_lo + out_rows, :].astype(F32)
        o_ref[q_lo:q_lo + out_rows, :] = (o_rows * (gb * _sigmoid(gb))).astype(o_ref.dtype)

    jobs = [(0, 0)] + [(t, j) for _, t, j in sorted(
        ((j + 0.5) / tiles[t][2], t, j) for t in range(1, len(tiles)) for j in range(tiles[t][2]))]
    cms_next = scores(*jobs[0], 0)
    for n, (t, j) in enumerate(jobs):
        cms = cms_next
        if n + 1 < len(jobs):
            cms_next = scores(*jobs[n + 1], (n + 1) % 2)
        fold(t, j, n % 2, cms)
        if j == max(tiles[t][2] - 1, 0):
            finish(t)


def _attn_prompt(q, kb, vb, gb, slab, lam_vec, subln_col, layer, lam_init, batch, seq):
    m, width = q.shape
    n_tiles = (seq - N_META) // Q_TILE
    assert N_META + n_tiles * Q_TILE == seq
    tp = (2 * n_tiles + 1) * LANES
    to3 = lambda a: a.reshape(batch, seq, width)
    head = pl.BlockSpec((None, seq, DV), lambda h, b: (b, 0, h))
    out = pl.pallas_call(
        functools.partial(_attn_prompt_kernel, seq=seq, n_tiles=n_tiles, lam_init=lam_init),
        grid=(N_HEADS_B, batch),
        in_specs=[pl.BlockSpec((None, 4, DK), lambda h, b: (layer, 0, 0)),
                  head, head, head, head,
                  pl.BlockSpec((None, seq + N_META, Q_TILE), lambda h, b: (h, 0, 0)),
                  pl.BlockSpec((None, DV, 1), lambda h, b: (layer, 0, 0))],
        out_specs=head,
        out_shape=jax.ShapeDtypeStruct((batch, seq, width), BF16),
        scratch_shapes=[pltpu.VMEM((2 * DK, tp), BF16),
                        pltpu.VMEM((2 * DK, tp), BF16),
                        pltpu.VMEM((DV, tp), BF16),
                        pltpu.VMEM((n_tiles + 1, 2, DV, Q_TILE), F32),
                        pltpu.VMEM((2, 2, N_META + KEY_BLOCK, Q_TILE), F32)],
        compiler_params=pltpu.CompilerParams(
            dimension_semantics=("arbitrary", "arbitrary"), vmem_limit_bytes=VMEM_LIMIT_BYTES),
        name="diff_attention_prompt",
    )(lam_vec, to3(q), to3(kb), to3(vb), to3(gb), slab, subln_col)
    return out.reshape(m, width)


def _attn_sample_kernel(lv_ref, q_ref, kc_ref, vc_ref, kn_ref, vn_ref, gb_ref, bs_ref, g_ref, o_ref,
                        kb_s, vb_s, s_s, *, past, dec_seq, lam_init):
    lam = _diff_lambda(lv_ref[...], lam_init)
    width = past + LANES
    lanes = lambda h: slice(h * DV, (h + 1) * DV)
    lane = lax.broadcasted_iota(jnp.int32, (dec_seq, DV), 1)
    zero = jnp.zeros((dec_seq, DV), BF16)
    tail = jnp.zeros((LANES - dec_seq, DV), BF16)

    def scores(h, slot):
        for dst, cache, new in ((kb_s, kc_ref, kn_ref), (vb_s, vc_ref, vn_ref)):
            dst[slot, 0:past, :] = cache[pl.ds(h, past, stride=N_HEADS_B), :].astype(BF16)
            dst[slot, past:past + dec_seq, :] = new[:, lanes(h)]
            dst[slot, past + dec_seq:width, :] = tail
        qv = q_ref[:, lanes(h)]
        q2 = jnp.concatenate([jnp.where(lane < DK, qv, zero), jnp.where(lane >= DK, qv, zero)], axis=0)
        s = lax.dot_general(q2, kb_s[slot], (((1,), (1,)), ((), ())), preferred_element_type=F32)
        for m in range(2):
            s_s[slot, m] = s[m * dec_seq:(m + 1) * dec_seq] + bs_ref[h]

    def fold(h, slot):
        probs = []
        for m in range(2):
            s = s_s[slot, m]
            mx = jnp.max(s, axis=-1, keepdims=True)
            p = jnp.exp2(s - mx)
            probs.append(p * (1.0 / jnp.sum(p, axis=-1, keepdims=True)))
        pd = (probs[0] - lam * probs[1]).astype(BF16)
        o = jnp.dot(pd, vb_s[slot], preferred_element_type=F32)
        ms = jnp.mean(o * o, axis=-1, keepdims=True)
        on = ((o * lax.rsqrt(ms + EPS)) * g_ref[...]) * (1.0 - lam_init)
        gb = gb_ref[:, lanes(h)].astype(F32)
        o_ref[:, lanes(h)] = (on * (gb * _sigmoid(gb))).astype(o_ref.dtype)

    scores(0, 0)
    for h in range(N_HEADS_B):
        if h + 1 < N_HEADS_B:
            scores(h + 1, (h + 1) % 2)
        fold(h, h % 2)


def _attn_sample(q, kb, vb, gb, cache_k, cache_v, slab, lam_vec, subln_row, layer, lam_init,
                 batch, dec_seq):
    m, width = q.shape
    depth, _, past = cache_k.shape[:3]
    ck = cache_k.reshape(depth, batch, past * N_HEADS_B, DV)
    cv = cache_v.reshape(depth, batch, past * N_HEADS_B, DV)
    rows = pl.BlockSpec((dec_seq, width), lambda b: (b, 0))
    cache = pl.BlockSpec((None, None, past * N_HEADS_B, DV), lambda b: (layer, b, 0, 0))
    return pl.pallas_call(
        functools.partial(_attn_sample_kernel, past=past, dec_seq=dec_seq, lam_init=lam_init),
        grid=(batch,),
        in_specs=[pl.BlockSpec((None, 4, DK), lambda b: (layer, 0, 0)),
                  rows, cache, cache, rows, rows, rows,
                  pl.BlockSpec((N_HEADS_B, dec_seq, past + LANES), lambda b: (0, 0, 0)),
                  pl.BlockSpec((None, 1, DV), lambda b: (layer, 0, 0))],
        out_specs=rows,
        out_shape=jax.ShapeDtypeStruct((m, width), BF16),
        scratch_shapes=[pltpu.VMEM((2, past + LANES, DV), BF16),
                        pltpu.VMEM((2, past + LANES, DV), BF16),
                        pltpu.VMEM((2, 2, dec_seq, past + LANES), F32)],
        compiler_params=pltpu.CompilerParams(
            dimension_semantics=("arbitrary",), vmem_limit_bytes=VMEM_LIMIT_BYTES),
        name="diff_attention_sample",
    )(lam_vec, q, ck, cv, kb, vb, gb, slab, subln_row)


def _outproj_kernel(ya_ref, yb_ref, x_ref, w_ref, g_ref, o_ref, y_s, *, col_chunk):
    d_a = ya_ref.shape[-1]
    d = o_ref.shape[-1]
    for c in range(0, d, col_chunk):
        y_s[:, c:c + col_chunk] = (
            jnp.dot(ya_ref[...], w_ref[0:d_a, c:c + col_chunk], preferred_element_type=F32)
            + jnp.dot(yb_ref[...], w_ref[d_a:, c:c + col_chunk], preferred_element_type=F32))
    y = y_s[...]
    ms = jnp.mean(y * y, axis=-1, keepdims=True)
    o_ref[...] = x_ref[...] + (y * lax.rsqrt(ms + EPS)) * g_ref[...]


def _outproj(ya, yb, x2d, w_out_b, post_g, layer):
    m, d = x2d.shape
    tm = _row_tile(m, 384)
    half = pl.BlockSpec((tm, d // 2), lambda i: (i, 0))
    full = pl.BlockSpec((tm, d), lambda i: (i, 0))
    return pl.pallas_call(
        functools.partial(_outproj_kernel, col_chunk=PROJ_COLS),
        grid=(m // tm,),
        in_specs=[half, half, full,
                  pl.BlockSpec((None, d, d), lambda i: (layer, 0, 0)),
                  pl.BlockSpec((None, 1, d), lambda i: (layer, 0, 0))],
        out_specs=full,
        out_shape=jax.ShapeDtypeStruct((m, d), F32),
        scratch_shapes=[pltpu.VMEM((tm, d), F32)],
        compiler_params=pltpu.CompilerParams(
            dimension_semantics=("arbitrary",), vmem_limit_bytes=VMEM_LIMIT_BYTES),
        name="out_projection",
    )(ya, yb, x2d, w_out_b, post_g)


def _outproj_main_kernel(ya_ref, yb_ref, x_ref, w_ref, g_ref, o_ref, y_s, *, col_chunk):
    _outproj_kernel(ya_ref.at[0], yb_ref.at[0], x_ref.at[0], w_ref, g_ref, o_ref, y_s,
                    col_chunk=col_chunk)


def _outproj_main(ya, yb, x2d, w_out_b, post_g, layer, batch, seq):
    m, d = x2d.shape
    main = seq - N_META
    tm = _row_tile(main, 512)
    assert tm % N_META == 0
    row0 = lambda j: (j * (tm // N_META) + 1) * N_META
    win = lambda w: pl.BlockSpec((pl.Element(1), pl.Element(tm), pl.Element(w)),
                                 lambda b, j: (b, row0(j), 0))
    return pl.pallas_call(
        functools.partial(_outproj_main_kernel, col_chunk=PROJ_COLS),
        grid=(batch, main // tm),
        in_specs=[win(d // 2), win(d // 2), win(d),
                  pl.BlockSpec((None, d, d), lambda b, j: (layer, 0, 0)),
                  pl.BlockSpec((None, 1, d), lambda b, j: (layer, 0, 0))],
        out_specs=pl.BlockSpec((None, tm, d), lambda b, j: (b, j, 0)),
        out_shape=jax.ShapeDtypeStruct((batch, main, d), F32),
        scratch_shapes=[pltpu.VMEM((tm, d), F32)],
        compiler_params=pltpu.CompilerParams(
            dimension_semantics=("arbitrary", "arbitrary"), vmem_limit_bytes=VMEM_LIMIT_BYTES),
        name="out_projection_main",
    )(ya.reshape(batch, seq, d // 2), yb.reshape(batch, seq, d // 2), x2d.reshape(batch, seq, d),
      w_out_b, post_g)


def _block_diag_gates(w):
    depth, nb, bs, _ = w.shape
    per = GATE_BLOCK // bs
    w5 = w.reshape(depth, nb // per, per, bs, bs)
    eye = jnp.eye(per, dtype=w.dtype)
    bd = jnp.einsum('lgaij,ab->lgaibj', w5, eye)
    return bd.reshape(depth, nb // per, GATE_BLOCK, GATE_BLOCK).astype(BF16)


def kernel(x_prompt, x_sample, cache_k, cache_v, state_conv, state_rglru, meta, rel_bias, pre_g, post_g,
           w_in, conv_w, conv_b, gate_r_w, gate_r_b, gate_i_w, gate_i_b, rglru_lam, lam_q1, lam_k1,
           lam_q2, lam_k2, subln_g, w_out):
    batch, seq0, d = x_prompt.shape
    dec_batch, dec_seq, _ = x_sample.shape
    depth = w_in.shape[0]
    past = cache_k.shape[2]
    d_a = d // 2
    seq = seq0 + N_META
    n_tiles = seq0 // Q_TILE

    hp = jnp.concatenate(
        [jnp.broadcast_to(meta.astype(x_prompt.dtype)[None], (batch, N_META, d)), x_prompt],
        axis=1).reshape(batch * seq, d)
    hs = x_sample.reshape(dec_batch * dec_seq, d)

    w_in_first = w_in[0:1].astype(BF16)
    w_in_b = None
    w_out_b = None
    wr_bd = _block_diag_gates(gate_r_w)
    wi_bd = _block_diag_gates(gate_i_w)
    pre_g3 = pre_g.reshape(depth, 1, d)
    post_g3 = post_g.reshape(depth, 1, d)
    conv_b3 = conv_b.reshape(depth, 1, d_a)
    br3 = gate_r_b.reshape(depth, 1, d_a)
    bi3 = gate_i_b.reshape(depth, 1, d_a)
    lam3 = rglru_lam.reshape(depth, 1, d_a)
    lam_vec = jnp.stack([lam_q1, lam_k1, lam_q2, lam_k2], axis=1)
    subln_col = subln_g.reshape(depth, DV, 1)
    subln_row = subln_g.reshape(depth, 1, DV)

    slab_p, slab_s = _bias_slabs(rel_bias, n_tiles, past, dec_seq)

    zero_buf = jnp.zeros((batch, SUBLANES, d_a), F32)
    zero_h = jnp.zeros((batch, 1, d_a), F32)
    pad_rows = SUBLANES - (CONV_W - 1)
    state_conv8 = jnp.pad(state_conv, ((0, 0), (0, 0), (pad_rows, 0), (0, 0)))

    kv_p = None
    kv_s = None
    cp_l, rp_l, cs_l, rs_l = [], [], [], []
    for l in range(depth):
        lam_init = 0.8 - 0.6 * math.exp(-0.3 * l)
        w_l = (w_in_first, 0) if l == 0 else (w_in_b, l)
        xa, ga, q, kb, vb, gb, k5, v5 = _inproj(hp, pre_g3, *w_l, l, kv_p, depth)
        kv_p = (k5, v5)
        casts = [(w_out, l, w_out_b)] + ([(w_in, l + 1, w_in_b)] if l + 1 < depth else [])
        ya, cp, rp, *cast = _rglru(xa, ga, zero_buf, zero_h, conv_w, conv_b3, wr_bd, br3, wi_bd, bi3,
                                   lam3, l, batch, seq, casts=casts)
        w_out_b = cast[0]
        if l + 1 < depth:
            w_in_b = cast[1]
        if l:
            w_l = (w_in_b, l)
        yb = _attn_prompt(q, kb, vb, gb, slab_p, lam_vec, subln_col, l, lam_init, batch, seq)
        if l + 1 < depth:
            hp = _outproj(ya, yb, hp, w_out_b, post_g3, l)
        else:
            y_prompt = _outproj_main(ya, yb, hp, w_out_b, post_g3, l, batch, seq)
        cp_l.append(cp)
        rp_l.append(rp)
        xa, ga, q, kb, vb, gb, k5, v5 = _inproj(hs, pre_g3, *w_l, l, kv_s, depth)
        kv_s = (k5, v5)
        ya, cs, rs = _rglru(xa, ga, state_conv8[l], state_rglru[l][:, None, :], conv_w, conv_b3,
                            wr_bd, br3, wi_bd, bi3, lam3, l, dec_batch, dec_seq)
        yb = _attn_sample(q, kb, vb, gb, cache_k, cache_v, slab_s, lam_vec, subln_row, l, lam_init,
                          dec_batch, dec_seq)
        hs = _outproj(ya, yb, hs, w_out_b, post_g3, l)
        cs_l.append(cs)
        rs_l.append(rs)

    y_sample = hs.reshape(dec_batch, dec_seq, d)
    k_prompt = kv_p[0].reshape(depth, batch, seq, N_HEADS_B, 2 * DK)
    v_prompt = kv_p[1].reshape(depth, batch, seq, N_HEADS_B, DV)
    k_sample = kv_s[0].reshape(depth, dec_batch, dec_seq, N_HEADS_B, 2 * DK)
    v_sample = kv_s[1].reshape(depth, dec_batch, dec_seq, N_HEADS_B, DV)
    return (y_prompt, y_sample, k_prompt, v_prompt, jnp.stack(cp_l), jnp.stack(rp_l),
            k_sample, v_sample, jnp.stack(cs_l), jnp.stack(rs_l))
```

```python
import functools
import math

import jax
import jax.numpy as jnp
from jax import lax
from jax.experimental import pallas as pl
from jax.experimental.pallas import tpu as pltpu

F32 = jnp.float32
BF16 = jnp.bfloat16

CHUNK = 64
N_META = 16
N_BLOCKS_A = 16
CONV_W = 4
RG_C = 8.0
N_HEADS_B = 8
DK = 64
DV = 2 * DK
NUM_BUCKETS = 32
REL_MAX_DIST = 1024
EPS = 1e-6

LANES = 128
SUBLANES = 8
BF16_ROWS = 16
VMEM_LIMIT_BYTES = 56 * 1024 * 1024

LOG2E = math.log2(math.e)
Q_TILE = 256
KEY_BLOCK = 2048
PROJ_COLS = 512
FAR_REL = math.ceil((NUM_BUCKETS // 4) * (REL_MAX_DIST / (NUM_BUCKETS // 4))
                    ** ((NUM_BUCKETS // 4 - 1) / (NUM_BUCKETS // 4))) + CHUNK
GATE_BLOCK = 256
RGLRU_SCRATCH_BYTES = 16 * 1024 * 1024


def _sigmoid(x):
    return 1.0 / (1.0 + jnp.exp(-x))


def _row_tile(m, cap=512):
    best = None
    for t in range(BF16_ROWS, min(m, cap) + 1, BF16_ROWS):
        if m % t == 0:
            best = t
    assert best is not None, m
    return best


def _bias_lookup(rel, tab_row):
    half = NUM_BUCKETS // 2
    max_exact = half // 2
    ret = jnp.where(rel > 0, half, 0).astype(jnp.int32)
    n = jnp.abs(rel)
    nf = jnp.maximum(n, 1).astype(F32)
    large = max_exact + (jnp.log(nf / max_exact) / math.log(REL_MAX_DIST / max_exact)
                         * (half - max_exact)).astype(jnp.int32)
    large = jnp.minimum(large, half - 1)
    bucket = ret + jnp.where(n < max_exact, n, large)
    tab = jnp.broadcast_to(tab_row, (rel.shape[0], LANES))
    return jnp.concatenate(
        [jnp.take_along_axis(tab, bucket[:, c:c + LANES], axis=1) for c in range(0, rel.shape[1], LANES)],
        axis=1)


def _prompt_slab_kernel(tab_ref, out_ref, *, n_tiles):
    seq = Q_TILE * n_tiles + N_META
    diag0 = Q_TILE * (n_tiles - 1) + N_META
    far_bucket = NUM_BUCKETS // 2 - 1
    tab_row = tab_ref[...] - tab_ref[:, far_bucket:far_bucket + 1]
    r0 = 0
    while r0 < seq:
        blk = min(LANES, seq - r0)
        r = lax.broadcasted_iota(jnp.int32, (blk, Q_TILE), 0) + r0
        c = lax.broadcasted_iota(jnp.int32, (blk, Q_TILE), 1)
        val = _bias_lookup(r - c - diag0, tab_row) * LOG2E
        rp = r - diag0
        masked = jnp.logical_and(rp >= 0, (rp // CHUNK) > (c // CHUNK))
        out_ref[r0:r0 + blk, :] = jnp.where(masked, -jnp.inf, val)
        r0 += blk
    k = lax.broadcasted_iota(jnp.int32, (N_META, Q_TILE), 0)
    c = lax.broadcasted_iota(jnp.int32, (N_META, Q_TILE), 1)
    out_ref[seq:seq + N_META, :] = _bias_lookup(k - c, tab_row) * LOG2E


def _sample_slab_kernel(tab_ref, out_ref, *, past, dec_seq):
    width = out_ref.shape[-1]
    for c0 in range(0, width, 512):
        w = min(512, width - c0)
        t = lax.broadcasted_iota(jnp.int32, (dec_seq, w), 0)
        k = lax.broadcasted_iota(jnp.int32, (dec_seq, w), 1) + c0
        val = _bias_lookup(k - past - t, tab_ref[...]) * LOG2E
        out_ref[:, c0:c0 + w] = jnp.where(k >= past + dec_seq, -jnp.inf, val)


def _bias_slabs(rel_bias, n_tiles, past, dec_seq):
    rows = Q_TILE * n_tiles + 2 * N_META
    tab = jnp.pad(rel_bias.T, ((0, 0), (0, LANES - NUM_BUCKETS)))[:, None, :]
    tab_spec = pl.BlockSpec((None, 1, LANES), lambda h: (h, 0, 0))
    prompt = pl.pallas_call(
        functools.partial(_prompt_slab_kernel, n_tiles=n_tiles),
        grid=(N_HEADS_B,),
        in_specs=[tab_spec],
        out_specs=pl.BlockSpec((None, rows, Q_TILE), lambda h: (h, 0, 0)),
        out_shape=jax.ShapeDtypeStruct((N_HEADS_B, rows, Q_TILE), F32),
        name="prompt_bias_slab",
    )(tab)
    width = past + LANES
    sample = pl.pallas_call(
        functools.partial(_sample_slab_kernel, past=past, dec_seq=dec_seq),
        grid=(N_HEADS_B,),
        in_specs=[tab_spec],
        out_specs=pl.BlockSpec((None, dec_seq, width), lambda h: (h, 0, 0)),
        out_shape=jax.ShapeDtypeStruct((N_HEADS_B, dec_seq, width), F32),
        name="sample_bias_slab",
    )(tab)
    return prompt, sample


def _inproj_kernel(x_ref, g_ref, w_ref, *rest, col_chunk):
    xa_ref, ga_ref, q_ref, kb_ref, vb_ref, gb_ref, k5_ref, v5_ref, u_s = rest[-9:]
    tm = x_ref.shape[0]

    def normalise(lo, n):
        x = x_ref[lo:lo + n, :]
        ms = jnp.mean(x * x, axis=-1, keepdims=True)
        u_s[lo:lo + n, :] = ((x * lax.rsqrt(ms + EPS)) * g_ref[...]).astype(BF16)

    chunks = []
    col = 0
    for ref, scale, ref5 in ((xa_ref, None, None), (ga_ref, None, None),
                             (q_ref, (DK ** -0.5) * LOG2E, None),
                             (kb_ref, None, k5_ref), (vb_ref, None, v5_ref), (gb_ref, None, None)):
        for c in range(0, ref.shape[-1], col_chunk):
            chunks.append((ref, scale, ref5, col + c, c))
        col += ref.shape[-1]

    def project(lo, n, chunk):
        ref, scale, ref5, wcol, c = chunk
        acc = jnp.dot(u_s[lo:lo + n, :], w_ref[:, wcol:wcol + col_chunk], preferred_element_type=F32)
        if scale is not None:
            acc = acc * scale
        ref[lo:lo + n, c:c + col_chunk] = acc.astype(ref.dtype)
        if ref5 is not None:
            for j in range(col_chunk // DV):
                head = c // DV + j
                ref5[pl.ds(lo * N_HEADS_B + head, n, stride=N_HEADS_B), :] = acc[:, j * DV:(j + 1) * DV]

    normalise(0, tm)
    for chunk in chunks:
        project(0, tm, chunk)


def _inproj(x2d, pre_g, w_in_b, w_layer, layer, kv_stacks, depth):
    m, d = x2d.shape
    d_a = d // 2
    width = N_HEADS_B * DV
    in_cols = w_in_b.shape[-1]
    tm = _row_tile(m, 384)
    row = lambda w: pl.BlockSpec((tm, w), lambda i: (i, 0))
    stack = pl.BlockSpec((None, tm * N_HEADS_B, DV), lambda i: (layer, i, 0))
    in_specs = [
        row(d),
        pl.BlockSpec((None, 1, d), lambda i: (layer, 0, 0)),
        pl.BlockSpec((None, d, in_cols), lambda i: (w_layer, 0, 0), pipeline_mode=pl.Buffered(1)),
    ]
    args = [x2d, pre_g, w_in_b]
    aliases = {}
    if kv_stacks is not None:
        in_specs += [pl.BlockSpec(memory_space=pl.ANY)] * 2
        args += list(kv_stacks)
        aliases = {3: 6, 4: 7}
    out_shape = [
        jax.ShapeDtypeStruct((m, d_a), F32),
        jax.ShapeDtypeStruct((m, d_a), F32),
        jax.ShapeDtypeStruct((m, width), BF16),
        jax.ShapeDtypeStruct((m, width), BF16),
        jax.ShapeDtypeStruct((m, width), BF16),
        jax.ShapeDtypeStruct((m, width), BF16),
        jax.ShapeDtypeStruct((depth, m * N_HEADS_B, DV), F32),
        jax.ShapeDtypeStruct((depth, m * N_HEADS_B, DV), F32),
    ]
    out_specs = [row(d_a), row(d_a), row(width), row(width), row(width), row(width), stack, stack]
    return pl.pallas_call(
        functools.partial(_inproj_kernel, col_chunk=PROJ_COLS),
        grid=(m // tm,),
        in_specs=in_specs,
        out_specs=out_specs,
        out_shape=out_shape,
        scratch_shapes=[pltpu.VMEM((tm, d), BF16)],
        input_output_aliases=aliases,
        compiler_params=pltpu.CompilerParams(
            dimension_semantics=("arbitrary",), vmem_limit_bytes=VMEM_LIMIT_BYTES),
        name="in_projection",
    )(*args)


def _rglru_kernel(xa_ref, ga_ref, buf_ref, h0_ref, cw_ref, cb_ref, wr_ref, br_ref, wi_ref, bi_ref,
                  lam_ref, *rest, seq, seg, g_block, n_cast):
    xp_s, a_s, b_s, hs_s, as_s, hn_s = rest[-6:]
    outs = rest[-6 - 3 - n_cast:-6]
    y_ref, nbuf_ref, hl_ref = outs[:3]
    for src_ref, dst_ref in zip(rest[:n_cast], outs[3:]):
        dst_ref[...] = src_ref[...].astype(dst_ref.dtype)
    c = xa_ref.shape[-1]
    nl = c // LANES
    lanes = lambda j: slice(j * LANES, (j + 1) * LANES)
    pad = SUBLANES * seg - seq
    for j in range(nl):
        xp_s[j, 0:SUBLANES, :] = buf_ref[:, lanes(j)]
        xp_s[j, SUBLANES:SUBLANES + seq, :] = xa_ref[:, lanes(j)]
        if pad:
            xp_s[j, SUBLANES + seq:, :] = jnp.zeros((pad, LANES), F32)
        nbuf_ref[:, lanes(j)] = xp_s[j, seq:seq + SUBLANES, :]

    neg = -lam_ref[...]
    softplus = jnp.maximum(neg, 0.0) + jnp.log1p(jnp.exp(-jnp.abs(neg)))
    cneg = -RG_C * softplus
    cb = cb_ref[...]
    taps = [cw_ref[j:j + 1, :] for j in range(CONV_W)]

    cache = {}

    def rows_at(g):
        if g not in cache:
            cache[g] = jnp.concatenate(
                [xp_s[j, pl.ds(SUBLANES + g, SUBLANES, stride=seg), :] for j in range(nl)], axis=1)
        return cache[g]

    for g0 in range(0, seg, g_block):
        pieces = []
        for g in range(g0, g0 + g_block):
            xc = cb
            for j in range(CONV_W):
                xc = xc + rows_at(g - (CONV_W - 1) + j) * taps[j]
            pieces.append(xc)
        xc = jnp.concatenate(pieces, axis=0)
        xcb = xc.astype(BF16)

        def gate(w_ref, b_ref):
            z = [jnp.dot(xcb[:, k * GATE_BLOCK:(k + 1) * GATE_BLOCK], w_ref[k],
                         preferred_element_type=F32) for k in range(c // GATE_BLOCK)]
            return _sigmoid(jnp.concatenate(z, axis=1) + b_ref[...])

        r = gate(wr_ref, br_ref)
        ig = gate(wi_ref, bi_ref)
        log_a = r * cneg
        a = jnp.exp(log_a)
        lo, hi = g0 * SUBLANES, (g0 + g_block) * SUBLANES
        a_s[lo:hi, :] = a
        e = -jnp.tanh(log_a) * (a * a + 1.0)
        root = jnp.where(e > 0.0, e * lax.rsqrt(e), 0.0)
        b_s[lo:hi, :] = root * (ig * xc)

    rowid = lax.broadcasted_iota(jnp.int32, (SUBLANES, c), 0)
    first = rowid == 0

    def scan_body(g, carry):
        h, acc = carry
        r = pl.multiple_of(g * SUBLANES, SUBLANES)
        a = a_s[pl.ds(r, SUBLANES), :]
        h = a * h + b_s[pl.ds(r, SUBLANES), :]
        acc = a * acc
        hs_s[pl.ds(r, SUBLANES), :] = h
        as_s[pl.ds(r, SUBLANES), :] = acc
        return h, acc

    h_init = jnp.where(first, jnp.broadcast_to(h0_ref[...], (SUBLANES, c)), 0.0)
    end_h, end_a = lax.fori_loop(0, seg, scan_body, (h_init, jnp.ones((SUBLANES, c), F32)), unroll=8)

    d = jnp.zeros((SUBLANES, c), F32)
    for _ in range(SUBLANES - 1):
        d = jnp.where(first, 0.0, pltpu.roll(end_h + end_a * d, 1, 0))

    def fix_body(g, carry):
        r = pl.multiple_of(g * SUBLANES, SUBLANES)
        h = hs_s[pl.ds(r, SUBLANES), :] + as_s[pl.ds(r, SUBLANES), :] * d
        for j in range(nl):
            hn_s[j, pl.ds(g, SUBLANES, stride=seg), :] = h[:, lanes(j)]
        return carry

    lax.fori_loop(0, seg, fix_body, 0, unroll=8)

    for j in range(nl):
        ga = ga_ref[:, lanes(j)]
        y_ref[:, lanes(j)] = (hn_s[j, 0:seq, :] * (ga * _sigmoid(ga))).astype(y_ref.dtype)
        hl_ref[:, lanes(j)] = hn_s[j, seq - 1:seq, :]


def _rglru(xa, ga, buf8, h0, conv_w, conv_b, wr_bd, br, wi_bd, bi, lam, layer, batch, seq, casts=()):
    m, d_a = xa.shape
    xa3 = xa.reshape(batch, seq, d_a)
    ga3 = ga.reshape(batch, seq, d_a)
    seg = pl.cdiv(pl.cdiv(seq, SUBLANES), SUBLANES) * SUBLANES
    g_block = max(t for t in range(1, min(seg, 44) + 1) if seg % t == 0)
    cw = max(w for w in range(GATE_BLOCK, d_a + 1, GATE_BLOCK)
             if d_a % w == 0 and (w == GATE_BLOCK or 6 * 4 * SUBLANES * seg * w <= RGLRU_SCRATCH_BYTES))
    nc = d_a // cw
    nl = cw // LANES
    chan = lambda rows: pl.BlockSpec((None, rows, cw), lambda b, c: (b, 0, c))
    par = lambda rows: pl.BlockSpec((None, rows, cw), lambda b, c: (layer, 0, c))
    wspec = pl.BlockSpec((None, cw // GATE_BLOCK, GATE_BLOCK, GATE_BLOCK), lambda b, c: (layer, c, 0, 0))
    in_specs = [chan(seq), chan(seq), chan(SUBLANES), chan(1), par(CONV_W), par(1),
                wspec, par(1), wspec, par(1), par(1)]
    args = [xa3, ga3, buf8, h0, conv_w, conv_b, wr_bd, br, wi_bd, bi, lam]
    out_specs = [chan(seq), chan(SUBLANES), chan(1)]
    out_shape = [jax.ShapeDtypeStruct((batch, seq, d_a), BF16),
                 jax.ShapeDtypeStruct((batch, SUBLANES, d_a), F32),
                 jax.ShapeDtypeStruct((batch, 1, d_a), F32)]
    steps = batch * nc
    aliases = {}
    for w, src_layer, _ in casts:
        rows = w.shape[1] // steps
        assert rows * steps == w.shape[1] and rows % BF16_ROWS == 0, (w.shape, steps)
        slab = pl.BlockSpec((None, rows, w.shape[2]), lambda b, c, sl=src_layer: (sl, b * nc + c, 0))
        in_specs.append(slab)
        args.append(w)
        out_specs.append(slab)
        out_shape.append(jax.ShapeDtypeStruct(w.shape, BF16))
    for k, (_, _, stack) in enumerate(casts):
        if stack is not None:
            aliases[len(args)] = 3 + k
            in_specs.append(pl.BlockSpec(memory_space=pl.ANY))
            args.append(stack)
    y, nbuf, hl, *filled = pl.pallas_call(
        functools.partial(_rglru_kernel, seq=seq, seg=seg, g_block=g_block, n_cast=len(casts)),
        grid=(batch, nc),
        in_specs=in_specs,
        out_specs=out_specs,
        out_shape=out_shape,
        input_output_aliases=aliases,
        scratch_shapes=[pltpu.VMEM((nl, SUBLANES * (seg + 1), LANES), F32),
                        pltpu.VMEM((SUBLANES * seg, cw), F32),
                        pltpu.VMEM((SUBLANES * seg, cw), F32),
                        pltpu.VMEM((SUBLANES * seg, cw), F32),
                        pltpu.VMEM((SUBLANES * seg, cw), F32),
                        pltpu.VMEM((nl, SUBLANES * seg, LANES), F32)],
        compiler_params=pltpu.CompilerParams(
            dimension_semantics=("arbitrary", "arbitrary"), vmem_limit_bytes=VMEM_LIMIT_BYTES),
        name="rglru_mixer",
    )(*args)
    return (y.reshape(m, d_a), nbuf[:, SUBLANES - (CONV_W - 1):, :], hl[:, 0, :], *filled)


def _diff_lambda(lv, lam_init):
    s1 = jnp.sum(lv[0:1, :] * lv[1:2, :], axis=-1, keepdims=True)
    s2 = jnp.sum(lv[2:3, :] * lv[3:4, :], axis=-1, keepdims=True)
    return jnp.exp(s1) - jnp.exp(s2) + lam_init


def _attn_prompt_kernel(lv_ref, q_ref, k_ref, v_ref, gb_ref, bt_ref, g_ref, o_ref,
                        q1t_s, q2t_s, vt_s, acc_s, s_s, *, seq, n_tiles, lam_init):
    gap = LANES - N_META
    lam = _diff_lambda(lv_ref[...], lam_init)

    row = lax.broadcasted_iota(jnp.int32, (2 * DK, LANES), 0)
    zero = jnp.zeros((2 * DK, LANES), BF16)
    pad = jnp.zeros((gap, DV), BF16)
    for j in range(2 * n_tiles + 1):
        cols = slice(LANES * j, LANES * (j + 1))
        if j == 0:
            qb = jnp.concatenate([q_ref[0:N_META, :], pad], axis=0)
            vb = jnp.concatenate([v_ref[0:N_META, :], pad], axis=0)
        else:
            rows = slice(N_META + LANES * (j - 1), N_META + LANES * j)
            qb = q_ref[rows, :]
            vb = v_ref[rows, :]
        qt = qb.T
        q1t_s[:, cols] = jnp.where(row < DK, qt, zero)
        q2t_s[:, cols] = jnp.where(row >= DK, qt, zero)
        vt_s[:, cols] = vb.T

    gcol = g_ref[...] * (1.0 - lam_init)
    meta_pad = jnp.zeros((gap, Q_TILE), BF16)

    def first_main(i):
        return Q_TILE * (i + 1) - KEY_BLOCK * ((Q_TILE * (i + 1) - 1) // KEY_BLOCK)

    tiles = [(0, LANES, 0, seq, 0, N_META)]
    for i in range(n_tiles):
        tiles.append((LANES + Q_TILE * i, Q_TILE, 1 + (Q_TILE * (i + 1) - first_main(i)) // KEY_BLOCK,
                      Q_TILE * (n_tiles - 1 - i), N_META + Q_TILE * i, Q_TILE))
    q_maps = (q1t_s, q2t_s)
    state = {}

    def key_rows(t, j):
        first = first_main(t - 1) if t else 0
        if j == 0:
            return 0, N_META + first
        return N_META + first + KEY_BLOCK * (j - 1), KEY_BLOCK

    def scores(t, j, slot):
        q_col, n_q, _, b_lo, _, _ = tiles[t]
        r0, nr = key_rows(t, j)
        n_far = 0
        if t > 0:
            n_far = min(nr, max(0, tiles[t][4] - FAR_REL + 1 - r0)) // BF16_ROWS * BF16_ROWS
        cms = []
        s_both = jnp.dot(k_ref[r0:r0 + nr, :],
                         jnp.concatenate([qm[:, q_col:q_col + n_q] for qm in q_maps], axis=1),
                         preferred_element_type=F32)
        for m in range(2):
            s = s_both[:, m * n_q:(m + 1) * n_q]
            parts = []
            if n_far:
                parts.append(s[0:n_far])
                s_s[slot, m, 0:n_far, 0:n_q] = parts[-1]
            if n_far < nr:
                parts.append(s[n_far:nr] + bt_ref[b_lo + r0 + n_far:b_lo + r0 + nr, 0:n_q])
                s_s[slot, m, n_far:nr, 0:n_q] = parts[-1]
            cm = [jnp.max(p, axis=0, keepdims=True) for p in parts]
            cms.append(cm[0] if len(cm) == 1 else jnp.maximum(cm[0], cm[1]))
        return cms

    def fold(t, j, slot, cms):
        _, n_q, _, _, _, _ = tiles[t]
        r0, nr = key_rows(t, j)
        ops = []
        for m in range(2):
            if j == 0:
                m_new, alpha = cms[m], None
                s = s_s[slot, m, 0:nr, 0:n_q]
                p = jnp.exp2(s - m_new)
                den = jnp.sum(p, axis=0, keepdims=True)
                pb = p.astype(BF16)
                parts = [pb[0:N_META], meta_pad[:, 0:n_q]]
                if nr > N_META:
                    parts.append(pb[N_META:nr])
                pb = jnp.concatenate(parts, axis=0)
                c0, nc = 0, LANES + nr - N_META
            else:
                mx, den = state[t, m]
                m_new = jnp.maximum(mx, cms[m])
                alpha = jnp.exp2(mx - m_new)
                p = jnp.exp2(s_s[slot, m, 0:nr, 0:n_q] - m_new)
                den = alpha * den + jnp.sum(p, axis=0, keepdims=True)
                pb = p.astype(BF16)
                c0, nc = LANES + r0 - N_META, nr
            state[t, m] = (m_new, den)
            ops.append((alpha, pb, c0, nc))
        _, _, c0, nc = ops[0]
        o_both = jnp.dot(vt_s[:, c0:c0 + nc], jnp.concatenate([ops[0][1], ops[1][1]], axis=1),
                         preferred_element_type=F32)
        for m, (alpha, _, _, _) in enumerate(ops):
            o = o_both[:, m * n_q:(m + 1) * n_q]
            if alpha is None:
                acc_s[t, m, :, 0:n_q] = o
            else:
                acc_s[t, m, :, 0:n_q] = alpha * acc_s[t, m, :, 0:n_q] + o

    def finish(t):
        _, n_q, _, _, q_lo, out_rows = tiles[t]
        outs = [acc_s[t, m, :, 0:n_q] * (1.0 / state[t, m][1]) for m in range(2)]
        ot = outs[0] - lam * outs[1]
        ms = jnp.mean(ot * ot, axis=0, keepdims=True)
        on = (ot * lax.rsqrt(ms + EPS)) * gcol
        o_rows = on.T[0:out_rows, :]
        gb = gb_ref[q_lo:q_lo + out_rows, :].astype(F32)
        o_ref[q_lo:q_lo + out_rows, :] = (o_rows * (gb * _sigmoid(gb))).astype(o_ref.dtype)

    jobs = [(0, 0)] + [(t, j) for _, t, j in sorted(
        ((j + 0.5) / tiles[t][2], t, j) for t in range(1, len(tiles)) for j in range(tiles[t][2]))]
    cms_next = scores(*jobs[0], 0)
    for n, (t, j) in enumerate(jobs):
        cms = cms_next
        if n + 1 < len(jobs):
            cms_next = scores(*jobs[n + 1], (n + 1) % 2)
        fold(t, j, n % 2, cms)
        if j == max(tiles[t][2] - 1, 0):
            finish(t)


def _attn_prompt(q, kb, vb, gb, slab, lam_vec, subln_col, layer, lam_init, batch, seq):
    m, width = q.shape
    n_tiles = (seq - N_META) // Q_TILE
    assert N_META + n_tiles * Q_TILE == seq
    tp = (2 * n_tiles + 1) * LANES
    to3 = lambda a: a.reshape(batch, seq, width)
    head = pl.BlockSpec((None, seq, DV), lambda h, b: (b, 0, h))
    out = pl.pallas_call(
        functools.partial(_attn_prompt_kernel, seq=seq, n_tiles=n_tiles, lam_init=lam_init),
        grid=(N_HEADS_B, batch),
        in_specs=[pl.BlockSpec((None, 4, DK), lambda h, b: (layer, 0, 0)),
                  head, head, head, head,
                  pl.BlockSpec((None, seq + N_META, Q_TILE), lambda h, b: (h, 0, 0)),
                  pl.BlockSpec((None, DV, 1), lambda h, b: (layer, 0, 0))],
        out_specs=head,
        out_shape=jax.ShapeDtypeStruct((batch, seq, width), BF16),
        scratch_shapes=[pltpu.VMEM((2 * DK, tp), BF16),
                        pltpu.VMEM((2 * DK, tp), BF16),
                        pltpu.VMEM((DV, tp), BF16),
                        pltpu.VMEM((n_tiles + 1, 2, DV, Q_TILE), F32),
                        pltpu.VMEM((2, 2, N_META + KEY_BLOCK, Q_TILE), F32)],
        compiler_params=pltpu.CompilerParams(
            dimension_semantics=("arbitrary", "arbitrary"), vmem_limit_bytes=VMEM_LIMIT_BYTES),
        name="diff_attention_prompt",
    )(lam_vec, to3(q), to3(kb), to3(vb), to3(gb), slab, subln_col)
    return out.reshape(m, width)


def _attn_sample_kernel(lv_ref, q_ref, kc_ref, vc_ref, kn_ref, vn_ref, gb_ref, bs_ref, g_ref, o_ref,
                        kb_s, vb_s, s_s, *, past, dec_seq, lam_init):
    lam = _diff_lambda(lv_ref[...], lam_init)
    width = past + LANES
    lanes = lambda h: slice(h * DV, (h + 1) * DV)
    lane = lax.broadcasted_iota(jnp.int32, (dec_seq, DV), 1)
    zero = jnp.zeros((dec_seq, DV), BF16)
    tail = jnp.zeros((LANES - dec_seq, DV), BF16)

    def scores(h, slot):
        for dst, cache, new in ((kb_s, kc_ref, kn_ref), (vb_s, vc_ref, vn_ref)):
            dst[slot, 0:past, :] = cache[pl.ds(h, past, stride=N_HEADS_B), :].astype(BF16)
            dst[slot, past:past + dec_seq, :] = new[:, lanes(h)]
            dst[slot, past + dec_seq:width, :] = tail
        qv = q_ref[:, lanes(h)]
        q2 = jnp.concatenate([jnp.where(lane < DK, qv, zero), jnp.where(lane >= DK, qv, zero)], axis=0)
        s = lax.dot_general(q2, kb_s[slot], (((1,), (1,)), ((), ())), preferred_element_type=F32)
        for m in range(2):
            s_s[slot, m] = s[m * dec_seq:(m + 1) * dec_seq] + bs_ref[h]

    def fold(h, slot):
        probs = []
        for m in range(2):
            s = s_s[slot, m]
            mx = jnp.max(s, axis=-1, keepdims=True)
            p = jnp.exp2(s - mx)
            probs.append(p * (1.0 / jnp.sum(p, axis=-1, keepdims=True)))
        pd = (probs[0] - lam * probs[1]).astype(BF16)
        o = jnp.dot(pd, vb_s[slot], preferred_element_type=F32)
        ms = jnp.mean(o * o, axis=-1, keepdims=True)
        on = ((o * lax.rsqrt(ms + EPS)) * g_ref[...]) * (1.0 - lam_init)
        gb = gb_ref[:, lanes(h)].astype(F32)
        o_ref[:, lanes(h)] = (on * (gb * _sigmoid(gb))).astype(o_ref.dtype)

    scores(0, 0)
    for h in range(N_HEADS_B):
        if h + 1 < N_HEADS_B:
            scores(h + 1, (h + 1) % 2)
        fold(h, h % 2)


def _attn_sample(q, kb, vb, gb, cache_k, cache_v, slab, lam_vec, subln_row, layer, lam_init,
                 batch, dec_seq):
    m, width = q.shape
    depth, _, past = cache_k.shape[:3]
    ck = cache_k.reshape(depth, batch, past * N_HEADS_B, DV)
    cv = cache_v.reshape(depth, batch, past * N_HEADS_B, DV)
    rows = pl.BlockSpec((dec_seq, width), lambda b: (b, 0))
    cache = pl.BlockSpec((None, None, past * N_HEADS_B, DV), lambda b: (layer, b, 0, 0))
    return pl.pallas_call(
        functools.partial(_attn_sample_kernel, past=past, dec_seq=dec_seq, lam_init=lam_init),
        grid=(batch,),
        in_specs=[pl.BlockSpec((None, 4, DK), lambda b: (layer, 0, 0)),
                  rows, cache, cache, rows, rows, rows,
                  pl.BlockSpec((N_HEADS_B, dec_seq, past + LANES), lambda b: (0, 0, 0)),
                  pl.BlockSpec((None, 1, DV), lambda b: (layer, 0, 0))],
        out_specs=rows,
        out_shape=jax.ShapeDtypeStruct((m, width), BF16),
        scratch_shapes=[pltpu.VMEM((2, past + LANES, DV), BF16),
                        pltpu.VMEM((2, past + LANES, DV), BF16),
                        pltpu.VMEM((2, 2, dec_seq, past + LANES), F32)],
        compiler_params=pltpu.CompilerParams(
            dimension_semantics=("arbitrary",), vmem_limit_bytes=VMEM_LIMIT_BYTES),
        name="diff_attention_sample",
    )(lam_vec, q, ck, cv, kb, vb, gb, slab, subln_row)


def _outproj_kernel(ya_ref, yb_ref, x_ref, w_ref, g_ref, o_ref, y_s, *, col_chunk):
    d_a = ya_ref.shape[-1]
    d = o_ref.shape[-1]
    for c in range(0, d, col_chunk):
        y_s[:, c:c + col_chunk] = (
            jnp.dot(ya_ref[...], w_ref[0:d_a, c:c + col_chunk], preferred_element_type=F32)
            + jnp.dot(yb_ref[...], w_ref[d_a:, c:c + col_chunk], preferred_element_type=F32))
    y = y_s[...]
    ms = jnp.mean(y * y, axis=-1, keepdims=True)
    o_ref[...] = x_ref[...] + (y * lax.rsqrt(ms + EPS)) * g_ref[...]


def _outproj(ya, yb, x2d, w_out_b, post_g, layer):
    m, d = x2d.shape
    tm = _row_tile(m, 384)
    half = pl.BlockSpec((tm, d // 2), lambda i: (i, 0))
    full = pl.BlockSpec((tm, d), lambda i: (i, 0))
    return pl.pallas_call(
        functools.partial(_outproj_kernel, col_chunk=PROJ_COLS),
        grid=(m // tm,),
        in_specs=[half, half, full,
                  pl.BlockSpec((None, d, d), lambda i: (layer, 0, 0)),
                  pl.BlockSpec((None, 1, d), lambda i: (layer, 0, 0))],
        out_specs=full,
        out_shape=jax.ShapeDtypeStruct((m, d), F32),
        scratch_shapes=[pltpu.VMEM((tm, d), F32)],
        compiler_params=pltpu.CompilerParams(
            dimension_semantics=("arbitrary",), vmem_limit_bytes=VMEM_LIMIT_BYTES),
        name="out_projection",
    )(ya, yb, x2d, w_out_b, post_g)


def _outproj_main_kernel(ya_ref, yb_ref, x_ref, w_ref, g_ref, o_ref, y_s, *, col_chunk):
    _outproj_kernel(ya_ref.at[0], yb_ref.at[0], x_ref.at[0], w_ref, g_ref, o_ref, y_s,
                    col_chunk=col_chunk)


def _outproj_main(ya, yb, x2d, w_out_b, post_g, layer, batch, seq):
    m, d = x2d.shape
    main = seq - N_META
    tm = _row_tile(main, 512)
    assert tm % N_META == 0
    row0 = lambda j: (j * (tm // N_META) + 1) * N_META
    win = lambda w: pl.BlockSpec((pl.Element(1), pl.Element(tm), pl.Element(w)),
                                 lambda b, j: (b, row0(j), 0))
    return pl.pallas_call(
        functools.partial(_outproj_main_kernel, col_chunk=PROJ_COLS),
        grid=(batch, main // tm),
        in_specs=[win(d // 2), win(d // 2), win(d),
                  pl.BlockSpec((None, d, d), lambda b, j: (layer, 0, 0)),
                  pl.BlockSpec((None, 1, d), lambda b, j: (layer, 0, 0))],
        out_specs=pl.BlockSpec((None, tm, d), lambda b, j: (b, j, 0)),
        out_shape=jax.ShapeDtypeStruct((batch, main, d), F32),
        scratch_shapes=[pltpu.VMEM((tm, d), F32)],
        compiler_params=pltpu.CompilerParams(
            dimension_semantics=("arbitrary", "arbitrary"), vmem_limit_bytes=VMEM_LIMIT_BYTES),
        name="out_projection_main",
    )(ya.reshape(batch, seq, d // 2), yb.reshape(batch, seq, d // 2), x2d.reshape(batch, seq, d),
      w_out_b, post_g)


def _block_diag_gates(w):
    depth, nb, bs, _ = w.shape
    per = GATE_BLOCK // bs
    w5 = w.reshape(depth, nb // per, per, bs, bs)
    eye = jnp.eye(per, dtype=w.dtype)
    bd = jnp.einsum('lgaij,ab->lgaibj', w5, eye)
    return bd.reshape(depth, nb // per, GATE_BLOCK, GATE_BLOCK).astype(BF16)


def kernel(x_prompt, x_sample, cache_k, cache_v, state_conv, state_rglru, meta, rel_bias, pre_g, post_g,
           w_in, conv_w, conv_b, gate_r_w, gate_r_b, gate_i_w, gate_i_b, rglru_lam, lam_q1, lam_k1,
           lam_q2, lam_k2, subln_g, w_out):
    batch, seq0, d = x_prompt.shape
    dec_batch, dec_seq, _ = x_sample.shape
    depth = w_in.shape[0]
    past = cache_k.shape[2]
    d_a = d // 2
    seq = seq0 + N_META
    n_tiles = seq0 // Q_TILE

    hp = jnp.concatenate(
        [jnp.broadcast_to(meta.astype(x_prompt.dtype)[None], (batch, N_META, d)), x_prompt],
        axis=1).reshape(batch * seq, d)
    hs = x_sample.reshape(dec_batch * dec_seq, d)

    w_in_first = w_in[0:1].astype(BF16)
    w_in_b = None
    w_out_b = None
    wr_bd = _block_diag_gates(gate_r_w)
    wi_bd = _block_diag_gates(gate_i_w)
    pre_g3 = pre_g.reshape(depth, 1, d)
    post_g3 = post_g.reshape(depth, 1, d)
    conv_b3 = conv_b.reshape(depth, 1, d_a)
    br3 = gate_r_b.reshape(depth, 1, d_a)
    bi3 = gate_i_b.reshape(depth, 1, d_a)
    lam3 = rglru_lam.reshape(depth, 1, d_a)
    lam_vec = jnp.stack([lam_q1, lam_k1, lam_q2, lam_k2], axis=1)
    subln_col = subln_g.reshape(depth, DV, 1)
    subln_row = subln_g.reshape(depth, 1, DV)

    slab_p, slab_s = _bias_slabs(rel_bias, n_tiles, past, dec_seq)

    zero_buf = jnp.zeros((batch, SUBLANES, d_a), F32)
    zero_h = jnp.zeros((batch, 1, d_a), F32)
    pad_rows = SUBLANES - (CONV_W - 1)
    state_conv8 = jnp.pad(state_conv, ((0, 0), (0, 0), (pad_rows, 0), (0, 0)))

    kv_p = None
    kv_s = None
    cp_l, rp_l, cs_l, rs_l = [], [], [], []
    for l in range(depth):
        lam_init = 0.8 - 0.6 * math.exp(-0.3 * l)
        w_l = (w_in_first, 0) if l == 0 else (w_in_b, l)
        xa, ga, q, kb, vb, gb, k5, v5 = _inproj(hp, pre_g3, *w_l, l, kv_p, depth)
        kv_p = (k5, v5)
        casts = [(w_out, l, w_out_b)] + ([(w_in, l + 1, w_in_b)] if l + 1 < depth else [])
        ya, cp, rp, *cast = _rglru(xa, ga, zero_buf, zero_h, conv_w, conv_b3, wr_bd, br3, wi_bd, bi3,
                                   lam3, l, batch, seq, casts=casts)
        w_out_b = cast[0]
        if l + 1 < depth:
            w_in_b = cast[1]
        if l:
            w_l = (w_in_b, l)
        yb = _attn_prompt(q, kb, vb, gb, slab_p, lam_vec, subln_col, l, lam_init, batch, seq)
        if l + 1 < depth:
            hp = _outproj(ya, yb, hp, w_out_b, post_g3, l)
        else:
            y_prompt = _outproj_main(ya, yb, hp, w_out_b, post_g3, l, batch, seq)
        cp_l.append(cp)
        rp_l.append(rp)
        xa, ga, q, kb, vb, gb, k5, v5 = _inproj(hs, pre_g3, *w_l, l, kv_s, depth)
        kv_s = (k5, v5)
        ya, cs, rs = _rglru(xa, ga, state_conv8[l], state_rglru[l][:, None, :], conv_w, conv_b3,
                            wr_bd, br3, wi_bd, bi3, lam3, l, dec_batch, dec_seq)
        yb = _attn_sample(q, kb, vb, gb, cache_k, cache_v, slab_s, lam_vec, subln_row, l, lam_init,
                          dec_batch, dec_seq)
        hs = _outproj(ya, yb, hs, w_out_b, post_g3, l)
        cs_l.append(cs)
        rs_l.append(rs)

    y_sample = hs.reshape(dec_batch, dec_seq, d)
    k_prompt = kv_p[0].reshape(depth, batch, seq, N_HEADS_B, 2 * DK)
    v_prompt = kv_p[1].reshape(depth, batch, seq, N_HEADS_B, DV)
    k_sample = kv_s[0].reshape(depth, dec_batch, dec_seq, N_HEADS_B, 2 * DK)
    v_sample = kv_s[1].reshape(depth, dec_batch, dec_seq, N_HEADS_B, DV)
    return (y_prompt, y_sample, k_prompt, v_prompt, jnp.stack(cp_l), jnp.stack(rp_l),
            k_sample, v_sample, jnp.stack(cs_l), jnp.stack(rs_l))
```
